```python
import jax
import jax.numpy as jnp
from jax import lax
import numpy as np

D_MODEL = 2048
BATCH = 8
SEQ = 2048
DEPTH = 1

MIX_WIDTH = D_MODEL
RWKV_WIDTH = MIX_WIDTH // 2
RWKV_HEAD = 64
RWKV_HEADS = RWKV_WIDTH // RWKV_HEAD
DECAY_LORA = 64
AAA_LORA = 64
GATE_LORA = 160
RWKV_LN_EPS = 64e-5
ATT_WIDTH = MIX_WIDTH - RWKV_WIDTH
ATT_HEAD = 128
ATT_HEADS = ATT_WIDTH // ATT_HEAD
ATT_KV_HEADS = 2
IDX_HEADS = 16
IDX_HEAD = 64
TOPK_MAX = 256
Q_BLOCK = 128
ROPE_THETA = 500000.0
N_MEM = 256
CROSS_HEADS = 4
CROSS_HEAD = 128
D_FF = 4 * D_MODEL
NORM_EPS = 1e-5

RWKV_SPLITS = (RWKV_WIDTH, RWKV_WIDTH, RWKV_WIDTH, DECAY_LORA, AAA_LORA, GATE_LORA)
ATT_SPLITS = (ATT_WIDTH, ATT_KV_HEADS * ATT_HEAD, ATT_KV_HEADS * ATT_HEAD,
              IDX_HEADS * IDX_HEAD, IDX_HEAD, IDX_HEADS)
RWKV_COLS = sum(RWKV_SPLITS)
ATT_COLS = sum(ATT_SPLITS)
IN_COLS = RWKV_COLS + ATT_COLS

kernel_name = 'hybrid_rwkv7_dsa_block'


def split_cols(t, sizes):
    out, start = [], 0
    for s in sizes:
        out.append(t[..., start:start + s])
        start += s
    return out


def rms_norm(x, g):
    xf = x.astype(jnp.float32)
    y = xf * lax.rsqrt(jnp.mean(xf * xf, axis=-1, keepdims=True) + NORM_EPS)
    return (y * g.astype(jnp.float32)).astype(x.dtype)


def partial_rope(x, pos):
    dh = x.shape[-1]
    rot = dh // 4
    half = rot // 2
    inv_freq = ROPE_THETA ** (-jnp.arange(half, dtype=jnp.float32) / half)
    ang = pos.astype(jnp.float32)[..., None] * inv_freq
    cos = jnp.cos(ang)[:, :, None, :]
    sin = jnp.sin(ang)[:, :, None, :]
    xr = x[..., :rot].astype(jnp.float32)
    x1, x2 = xr[..., :half], xr[..., half:]
    rotated = jnp.concatenate([x1 * cos - x2 * sin, x2 * cos + x1 * sin], axis=-1).astype(x.dtype)
    return jnp.concatenate([rotated, x[..., rot:]], axis=-1)


def rwkv7_group(p, mu, w0, w_decay_up, a0, a_up, g_up, k_k, k_a, r_k, lnx_w, lnx_b):
    B, S, _ = p.shape
    H, N = RWKV_HEADS, RWKV_HEAD
    f32 = jnp.float32
    prev = jnp.pad(p, ((0, 0), (1, 0), (0, 0)))[:, :-1]
    p = p + (prev - p) * mu
    r, k, v, xw, xa, xg = split_cols(p, RWKV_SPLITS)
    w = -jax.nn.softplus(-(w0 + jnp.tanh(xw) @ w_decay_up)) - 0.5
    decay = jnp.exp(-jnp.exp(w.astype(f32)))
    a = jax.nn.sigmoid(a0 + xa @ a_up)
    g = jax.nn.sigmoid(xg) @ g_up
    heads = lambda t: t.astype(f32).reshape(B, S, H, N)
    kk = heads(k * k_k)
    kk = kk / jnp.maximum(jnp.sqrt(jnp.sum(kk * kk, axis=-1, keepdims=True)), 1e-12)
    k = k * (1.0 + (a - 1.0) * k_a)
    r4, k4, v4, a4, w4 = heads(r), heads(k), heads(v), heads(a), decay.reshape(B, S, H, N)

    def step(state, inp):
        r_t, w_t, k_t, v_t, kk_t, a_t = inp
        sa = jnp.einsum('bhvk,bhk->bhv', state, -kk_t)
        state = (state * w_t[:, :, None, :]
                 + sa[..., None] * (kk_t * a_t)[:, :, None, :]
                 + v_t[..., None] * k_t[:, :, None, :])
        return state, jnp.einsum('bhvk,bhk->bhv', state, r_t)

    tm = lambda t: jnp.swapaxes(t, 0, 1)
    s0 = jnp.zeros((B, H, N, N), f32)
    _, y = lax.scan(step, s0, (tm(r4), tm(w4), tm(k4), tm(v4), tm(kk), tm(a4)))
    y = tm(y)
    mean = jnp.mean(y, axis=-1, keepdims=True)
    var = jnp.mean(jnp.square(y - mean), axis=-1, keepdims=True)
    yn = ((y - mean) * lax.rsqrt(var + RWKV_LN_EPS)).reshape(B, S, RWKV_WIDTH)
    yn = yn * lnx_w.astype(f32) + lnx_b.astype(f32)
    bonus = (jnp.sum(r4 * k4 * r_k.astype(f32), axis=-1, keepdims=True) * v4).reshape(B, S, RWKV_WIDTH)
    return ((yn + bonus) * g.astype(f32)).astype(p.dtype)


def dsa_group(p, pos):
    B, S, _ = p.shape
    f32 = jnp.float32
    G = ATT_HEADS // ATT_KV_HEADS
    q, k, v, qi, ki, wi = split_cols(p, ATT_SPLITS)
    q = partial_rope(q.reshape(B, S, ATT_HEADS, ATT_HEAD), pos)
    k = partial_rope(k.reshape(B, S, ATT_KV_HEADS, ATT_HEAD), pos)
    v = v.reshape(B, S, ATT_KV_HEADS, ATT_HEAD)
    qi = partial_rope(qi.reshape(B, S, IDX_HEADS, IDX_HEAD), pos)
    ki = partial_rope(ki.reshape(B, S, 1, IDX_HEAD), pos)[:, :, 0].astype(f32)
    wi = wi.astype(f32) * (IDX_HEADS ** -0.5 * IDX_HEAD ** -0.5)
    n_sel = min(TOPK_MAX, S // 4)
    nblk = S // Q_BLOCK
    key_pos = jnp.arange(S)
    gather = jax.vmap(lambda t, i: t[i])

    def to_blocks(t):
        return jnp.swapaxes(t.reshape((B, nblk, Q_BLOCK) + t.shape[2:]), 0, 1)

    def block(args):
        bi, qb, qib, wib = args
        q_pos = bi * Q_BLOCK + jnp.arange(Q_BLOCK)
        rel = jax.nn.relu(jnp.einsum('bqhd,bsd->bqhs', qib.astype(f32), ki))
        score = jnp.einsum('bqhs,bqh->bqs', rel, wib)
        causal = key_pos[None, :] <= q_pos[:, None]
        score = jnp.where(causal[None], score, -1e30)
        _, idx = lax.top_k(score, n_sel)
        valid = idx <= q_pos[None, :, None]
        kg = gather(k, idx)
        vg = gather(v, idx)
        qg = qb.reshape(B, Q_BLOCK, ATT_KV_HEADS, G, ATT_HEAD)
        s = jnp.einsum('bqhgd,bqkhd->bqhgk', qg, kg).astype(f32) * (ATT_HEAD ** -0.5)
        s = jnp.where(valid[:, :, None, None, :], s, -jnp.inf)
        prob = jax.nn.softmax(s, axis=-1).astype(vg.dtype)
        o = jnp.einsum('bqhgk,bqkhd->bqhgd', prob, vg)
        return o.reshape(B, Q_BLOCK, ATT_WIDTH)

    out = lax.map(block, (jnp.arange(nblk), to_blocks(q), to_blocks(qi), to_blocks(wi)))
    return jnp.swapaxes(out, 0, 1).reshape(B, S, ATT_WIDTH)


def memory_cross_attention(hn, mem_n, w_q, w_kv, w_o):
    B, S, _ = hn.shape
    M = mem_n.shape[1]
    q = (hn @ w_q).reshape(B, S, CROSS_HEADS, CROSS_HEAD)
    k, v = split_cols(mem_n @ w_kv, (CROSS_HEADS * CROSS_HEAD, CROSS_HEADS * CROSS_HEAD))
    k = k.reshape(B, M, CROSS_HEADS, CROSS_HEAD)
    v = v.reshape(B, M, CROSS_HEADS, CROSS_HEAD)
    s = jnp.einsum('bshd,bmhd->bhsm', q, k).astype(jnp.float32) * (CROSS_HEAD ** -0.5)
    prob = jax.nn.softmax(s, axis=-1).astype(v.dtype)
    o = jnp.einsum('bhsm,bmhd->bshd', prob, v).reshape(B, S, CROSS_HEADS * CROSS_HEAD)
    return o @ w_o


def setup_inputs(seed: int = 0) -> dict:
    key = jax.random.key(seed)
    ks = jax.random.split(key, 32)
    f32 = jnp.float32
    L = DEPTH

    def dense(k, fan_in, shape):
        return jax.random.normal(k, shape, f32) * fan_in ** -0.5

    def gain(k, shape):
        return 1.0 + 0.02 * jax.random.normal(k, shape, f32)

    return {
        'x': jax.random.normal(ks[0], (BATCH, SEQ, D_MODEL), f32),
        'mem': jax.random.normal(ks[1], (BATCH, N_MEM, D_MODEL), f32),
        'positions': jnp.tile(jnp.arange(SEQ, dtype=jnp.int32)[None, :], (BATCH, 1)),
        'norm_mix': gain(ks[2], (L, D_MODEL)),
        'w_in': dense(ks[3], D_MODEL, (L, D_MODEL, IN_COLS)),
        'rwkv_mu': jax.random.uniform(ks[4], (L, RWKV_COLS), f32),
        'w_decay0': jax.random.uniform(ks[5], (L, RWKV_WIDTH), f32, -6.0, 1.0),
        'w_decay_up': dense(ks[6], DECAY_LORA, (L, DECAY_LORA, RWKV_WIDTH)),
        'a0': 0.1 * jax.random.normal(ks[7], (L, RWKV_WIDTH), f32),
        'a_up': dense(ks[8], AAA_LORA, (L, AAA_LORA, RWKV_WIDTH)),
        'g_up': dense(ks[9], GATE_LORA, (L, GATE_LORA, RWKV_WIDTH)),
        'k_k': 0.85 + 0.05 * jax.random.normal(ks[10], (L, RWKV_WIDTH), f32),
        'k_a': 1.0 + 0.05 * jax.random.normal(ks[11], (L, RWKV_WIDTH), f32),
        'r_k': 0.1 * jax.random.normal(ks[12], (L, RWKV_HEADS, RWKV_HEAD), f32),
        'lnx_w': gain(ks[13], (L, RWKV_WIDTH)),
        'lnx_b': 0.02 * jax.random.normal(ks[14], (L, RWKV_WIDTH), f32),
        'w_mix_out': dense(ks[15], MIX_WIDTH, (L, MIX_WIDTH, D_MODEL)),
        'norm_cross': gain(ks[16], (L, D_MODEL)),
        'norm_mem': gain(ks[17], (L, D_MODEL)),
        'w_q_cross': dense(ks[18], D_MODEL, (L, D_MODEL, CROSS_HEADS * CROSS_HEAD)),
        'w_kv_cross': dense(ks[19], D_MODEL, (L, D_MODEL, 2 * CROSS_HEADS * CROSS_HEAD)),
        'w_o_cross': dense(ks[20], CROSS_HEADS * CROSS_HEAD, (L, CROSS_HEADS * CROSS_HEAD, D_MODEL)),
        'norm_mlp': gain(ks[21], (L, D_MODEL)),
        'w_up': dense(ks[22], D_MODEL, (L, D_MODEL, D_FF)),
        'w_down': dense(ks[23], D_FF, (L, D_FF, D_MODEL)),
        'norm_final': gain(ks[24], (D_MODEL,)),
    }


def reference(x, mem, positions, norm_mix, w_in, rwkv_mu, w_decay0, w_decay_up, a0, a_up, g_up,
              k_k, k_a, r_k, lnx_w, lnx_b, w_mix_out, norm_cross, norm_mem, w_q_cross,
              w_kv_cross, w_o_cross, norm_mlp, w_up, w_down, norm_final):
    h = x
    for l in range(DEPTH):
        p = rms_norm(h, norm_mix[l]) @ w_in[l]
        y_rwkv = rwkv7_group(p[..., :RWKV_COLS], rwkv_mu[l], w_decay0[l], w_decay_up[l], a0[l],
                             a_up[l], g_up[l], k_k[l], k_a[l], r_k[l], lnx_w[l], lnx_b[l])
        y_att = dsa_group(p[..., RWKV_COLS:], positions)
        h = h + jnp.concatenate([y_rwkv, y_att], axis=-1) @ w_mix_out[l]
        h = h + memory_cross_attention(rms_norm(h, norm_cross[l]), rms_norm(mem, norm_mem[l]),
                                       w_q_cross[l], w_kv_cross[l], w_o_cross[l])
        u = rms_norm(h, norm_mlp[l]) @ w_up[l]
        h = h + jnp.square(jax.nn.relu(u)) @ w_down[l]
    return rms_norm(h, norm_final)
```

```python
import functools

import jax
import jax.numpy as jnp
from jax import lax
from jax.experimental import pallas as pl
from jax.experimental.pallas import tpu as pltpu

f32 = jnp.float32
bf16 = jnp.bfloat16

D_MODEL = 2048
RWKV_WIDTH = 1024
RWKV_HEAD = 64
DECAY_LORA = 64
AAA_LORA = 64
GATE_LORA = 160
RWKV_LN_EPS = 64e-5
ATT_WIDTH = 1024
ATT_HEAD = 128
ATT_HEADS = 8
ATT_KV_HEADS = 2
IDX_HEADS = 16
IDX_HEAD = 64
TOPK_MAX = 256
ROPE_THETA = 500000.0
CROSS_HEADS = 4
CROSS_HEAD = 128
D_FF = 4 * D_MODEL
NORM_EPS = 1e-5

LANES = 128
VMEM_LIMIT_BYTES = 56 * 1024 * 1024

COL_R, COL_K, COL_V, COL_Q, COL_QI = 0, 1024, 2048, 3072, 4096
COL_KATT, COL_VATT, COL_KIWI, COL_LORA = 5120, 5376, 5632, 5760
LORA_BLOCK = 384
IN_COLS_PACKED = 6144
RWKV_CHUNK = 64


def _nt(a, b):
    return lax.dot_general(a, b, (((1,), (1,)), ((), ())), preferred_element_type=f32)


def _mm(a, b):
    return jnp.dot(a, b, preferred_element_type=f32)


def _rms(x, g):
    ms = jnp.mean(x * x, axis=-1, keepdims=True)
    return x * lax.rsqrt(ms + NORM_EPS) * g


def _params(*sem):
    return pltpu.CompilerParams(dimension_semantics=sem, vmem_limit_bytes=VMEM_LIMIT_BYTES)


def _norm_matmul_kernel(x_ref, g_ref, w_ref, o_ref, xn_ref):
    @pl.when(pl.program_id(1) == 0)
    def _():
        xn_ref[...] = _rms(x_ref[...], g_ref[...]).astype(bf16)

    o_ref[...] = _mm(xn_ref[...], w_ref[...]).astype(o_ref.dtype)


def _norm_matmul(x, g, w, out_dtype, tm, tn):
    t, d = x.shape
    n = w.shape[1]
    return pl.pallas_call(
        _norm_matmul_kernel,
        grid=(t // tm, n // tn),
        in_specs=[
            pl.BlockSpec((tm, d), lambda i, j: (i, 0)),
            pl.BlockSpec((1, d), lambda i, j: (0, 0)),
            pl.BlockSpec((d, tn), lambda i, j: (0, j)),
        ],
        out_specs=pl.BlockSpec((tm, tn), lambda i, j: (i, j)),
        out_shape=jax.ShapeDtypeStruct((t, n), out_dtype),
        scratch_shapes=[pltpu.VMEM((tm, d), bf16)],
        compiler_params=_params("parallel", "arbitrary"),
        name="norm_matmul",
    )(x, g, w)


def _seg_sum(x, first_half):
    s_a = jnp.sum(jnp.where(first_half, x, 0.0), axis=-1, keepdims=True)
    s_b = jnp.sum(jnp.where(first_half, 0.0, x), axis=-1, keepdims=True)
    return jnp.where(first_half, s_a, s_b)


def _rwkv_kernel(r_ref, k_ref, v_ref, lo_ref, mur_ref, muk_ref, muv_ref, mulo_ref,
                 w0_ref, a0_ref, kkw_ref, kaw_ref, lnw_ref, lnb_ref, rk_ref, wwa_ref, gup_ref,
                 y_ref, st_ref, pr_ref, pk_ref, pv_ref, plo_ref):
    C = r_ref.shape[0]
    n_pairs = RWKV_WIDTH // LANES

    @pl.when(pl.program_id(1) == 0)
    def _():
        st_ref[...] = jnp.zeros_like(st_ref)
        pr_ref[...] = jnp.zeros_like(pr_ref)
        pk_ref[...] = jnp.zeros_like(pk_ref)
        pv_ref[...] = jnp.zeros_like(pv_ref)
        plo_ref[...] = jnp.zeros_like(plo_ref)

    row = lax.broadcasted_iota(jnp.int32, (C, 1), 0)

    def shifted(x_ref, prev_ref, mu_ref):
        x = x_ref[...]
        prev = jnp.where(row == 0, prev_ref[...], pltpu.roll(x, 1, 0))
        prev_ref[...] = x[C - 1:C, :]
        return x + (prev - x) * mu_ref[...]

    r = shifted(r_ref, pr_ref, mur_ref)
    k = shifted(k_ref, pk_ref, muk_ref)
    v = shifted(v_ref, pv_ref, muv_ref)
    lo = shifted(lo_ref, plo_ref, mulo_ref)

    lane = lax.broadcasted_iota(jnp.int32, (1, LANES), 1)
    first_half = lane < RWKV_HEAD

    wa_in = lo[:, :LANES]
    wa_in = jnp.where(first_half, jnp.tanh(wa_in), wa_in).astype(bf16)
    wa = _mm(wa_in, wwa_ref[...])
    zw = -(w0_ref[...] + wa[:, :RWKV_WIDTH])
    softplus = jnp.maximum(zw, 0.0) + jnp.log(1.0 + jnp.exp(-jnp.abs(zw)))
    logdecay = -jnp.exp(-softplus - 0.5)
    a = 1.0 / (1.0 + jnp.exp(-(a0_ref[...] + wa[:, RWKV_WIDTH:])))
    xg = lo[:, LANES:]
    gate = _mm((1.0 / (1.0 + jnp.exp(-xg))).astype(bf16), gup_ref[...])

    ti = lax.broadcasted_iota(jnp.int32, (C, C), 0)
    tj = lax.broadcasted_iota(jnp.int32, (C, C), 1)
    tri_incl = ti >= tj
    tri_strict = ti > tj
    tri_b = jnp.where(tri_incl, 1.0, 0.0).astype(bf16)
    ld_hi = logdecay.astype(bf16)
    ld_r1 = logdecay - ld_hi.astype(f32)
    ld_mid = ld_r1.astype(bf16)
    ld_lo = (ld_r1 - ld_mid.astype(f32)).astype(bf16)
    cum = _mm(tri_b, ld_hi) + _mm(tri_b, ld_mid) + _mm(tri_b, ld_lo)

    kk = k * kkw_ref[...]
    k2 = k * (1.0 + (a - 1.0) * kaw_ref[...])
    rkk = r * k2 * rk_ref[...]
    eye = jnp.where(ti == tj, 1.0, 0.0)
    vi = lax.broadcasted_iota(jnp.int32, (LANES, LANES), 0)
    vj = lax.broadcasted_iota(jnp.int32, (LANES, LANES), 1)
    same_head = (vi < RWKV_HEAD) == (vj < RWKV_HEAD)

    for p in range(n_pairs):
        sl = slice(p * LANES, (p + 1) * LANES)
        rp, k2p, vp, ap, kkp = r[:, sl], k2[:, sl], v[:, sl], a[:, sl], kk[:, sl]
        cum_p, ld_p = cum[:, sl], logdecay[:, sl]
        ss = _seg_sum(kkp * kkp, first_half)
        kkn = kkp / jnp.maximum(jnp.sqrt(ss), 1e-12)
        ecum = jnp.exp(cum_p)
        encum = jnp.exp(-cum_p)
        a_t = -kkn * jnp.exp(cum_p - ld_p)
        b_t = kkn * ap * encum
        k_t = k2p * encum
        r_t = rp * ecum
        g_end = ecum[C - 1:C, :]
        s0 = st_ref[p]
        ar0 = _nt(jnp.concatenate([a_t, r_t], axis=0).astype(bf16), s0.astype(bf16))
        a0s, r0s = ar0[:C], ar0[C:]
        b_b, k_b, v_b = b_t.astype(bf16), k_t.astype(bf16), vp.astype(bf16)
        us, mbs, mks = [], [], []
        for h in range(2):
            m = first_half if h == 0 else jnp.logical_not(first_half)
            a_h = jnp.where(m, a_t, 0.0).astype(bf16)
            r_h = jnp.where(m, r_t, 0.0).astype(bf16)
            l_b = jnp.where(tri_strict, _nt(a_h, b_b), 0.0)
            l_k = jnp.where(tri_strict, _nt(a_h, k_b), 0.0)
            mbs.append(jnp.where(tri_incl, _nt(r_h, b_b), 0.0).astype(bf16))
            mks.append(jnp.where(tri_incl, _nt(r_h, k_b), 0.0).astype(bf16))
            w_h = a0s + _mm(l_k.astype(bf16), v_b)
            inv = eye + l_b
            l_pow = l_b
            for _ in range(C.bit_length() - 2):
                l_pow_b = l_pow.astype(bf16)
                l_pow = _mm(l_pow_b, l_pow_b)
                inv = inv + _mm(inv.astype(bf16), l_pow.astype(bf16))
            us.append(_mm(inv.astype(bf16), w_h.astype(bf16)))
        u = jnp.where(first_half, us[0], us[1])
        u_b = u.astype(bf16)
        y_intra = jnp.where(first_half,
                            _mm(mbs[0], u_b) + _mm(mks[0], v_b),
                            _mm(mbs[1], u_b) + _mm(mks[1], v_b))
        y = r0s + y_intra
        uv_t = jnp.concatenate([u, vp], axis=0).T.astype(bf16)
        bk_end = (jnp.concatenate([b_t, k_t], axis=0) * g_end).astype(bf16)
        st_ref[p] = s0 * g_end + jnp.where(same_head, _mm(uv_t, bk_end), 0.0)

        mean = _seg_sum(y, first_half) * (1.0 / RWKV_HEAD)
        yc = y - mean
        var = _seg_sum(yc * yc, first_half) * (1.0 / RWKV_HEAD)
        yn = yc * lax.rsqrt(var + RWKV_LN_EPS) * lnw_ref[:, sl] + lnb_ref[:, sl]
        bonus = _seg_sum(rkk[:, sl], first_half) * vp
        y_ref[:, sl] = ((yn + bonus) * gate[:, sl]).astype(y_ref.dtype)


def _rwkv(p, batch, seq, mu, w0, a0, k_k, k_a, lnx_w, lnx_b, r_k, w_wa, g_up_p):
    C = RWKV_CHUNK
    nc = seq // C
    W = RWKV_WIDTH
    row_block = lambda width, col: pl.BlockSpec((C, width), lambda b, c: (b * nc + c, col // width))
    vec = lambda width: pl.BlockSpec((1, width), lambda b, c: (0, 0))
    full = lambda arr: pl.BlockSpec(arr.shape, lambda b, c: (0, 0))
    mu_r, mu_k, mu_v, mu_lo = mu
    return pl.pallas_call(
        _rwkv_kernel,
        grid=(batch, nc),
        in_specs=[row_block(W, COL_R), row_block(W, COL_K), row_block(W, COL_V),
                  row_block(LORA_BLOCK, COL_LORA),
                  vec(W), vec(W), vec(W), vec(LORA_BLOCK),
                  vec(W), vec(W), vec(W), vec(W), vec(W), vec(W), vec(W),
                  full(w_wa), full(g_up_p)],
        out_specs=pl.BlockSpec((C, W), lambda b, c: (b * nc + c, 0)),
        out_shape=jax.ShapeDtypeStruct((batch * seq, W), bf16),
        scratch_shapes=[pltpu.VMEM((W // LANES, LANES, LANES), f32),
                        pltpu.VMEM((1, W), f32), pltpu.VMEM((1, W), f32), pltpu.VMEM((1, W), f32),
                        pltpu.VMEM((1, LORA_BLOCK), f32)],
        compiler_params=_params("parallel", "arbitrary"),
        name="rwkv7_chunked",
    )(p, p, p, p, mu_r, mu_k, mu_v, mu_lo, w0, a0, k_k, k_a, lnx_w, lnx_b, r_k, w_wa, g_up_p)


def _rope(x, cos, sin_lo, sin_hi, half):
    n = x.shape[-1]
    return x * cos + pltpu.roll(x, n - half, 1) * sin_lo + pltpu.roll(x, half, 1) * sin_hi


def _dsa_prep_kernel(pos_ref, fa_ref, fi_ref, q_ref, qi_ref, ka_ref, va_ref, kiwi_ref,
                     qo_ref, qio_ref, ko_ref, vo_ref, ki2_ref, wi_ref):
    pos = pos_ref[...]
    lane = lax.broadcasted_iota(jnp.int32, (1, LANES), 1)

    def tables(freq, head, half):
        ang = pos * freq
        cos, sin = jnp.cos(ang), jnp.sin(ang)
        in_head = lane % head
        sin_lo = jnp.where(in_head < half, -sin, 0.0)
        sin_hi = jnp.where((in_head >= half) & (in_head < 2 * half), sin, 0.0)
        return cos, sin_lo, sin_hi

    ca, sla, sha = tables(fa_ref[...], ATT_HEAD, ATT_HEAD // 8)
    ci, sli, shi = tables(fi_ref[...], IDX_HEAD, IDX_HEAD // 8)
    for h in range(ATT_HEADS):
        sl = slice(h * LANES, (h + 1) * LANES)
        qo_ref[:, sl] = _rope(q_ref[:, sl], ca, sla, sha, ATT_HEAD // 8).astype(bf16)
        qio_ref[:, sl] = _rope(qi_ref[:, sl], ci, sli, shi, IDX_HEAD // 8).astype(bf16)
    for h in range(ATT_KV_HEADS):
        sl = slice(h * LANES, (h + 1) * LANES)
        ko_ref[:, sl] = _rope(ka_ref[:, sl], ca, sla, sha, ATT_HEAD // 8).astype(bf16)
    vo_ref[...] = va_ref[...].astype(bf16)
    kiwi = kiwi_ref[...]
    ki_only = jnp.where(lane < IDX_HEAD, kiwi, 0.0)
    ki = _rope(ki_only, ci, sli, shi, IDX_HEAD // 8)
    ki2_ref[...] = (ki + pltpu.roll(ki, IDX_HEAD, 1)).astype(bf16)
    wi = pltpu.roll(kiwi, LANES - IDX_HEAD, 1)
    wi_ref[...] = jnp.where(lane < IDX_HEADS, wi, 0.0) * (IDX_HEADS ** -0.5 * IDX_HEAD ** -0.5)


def _dsa_prep(p, pos, freq_att, freq_idx, tm):
    t = p.shape[0]
    blk = lambda width, col: pl.BlockSpec((tm, width), lambda i: (i, col // width))
    out = lambda width: pl.BlockSpec((tm, width), lambda i: (i, 0))
    vec = pl.BlockSpec((1, LANES), lambda i: (0, 0))
    kvw = ATT_KV_HEADS * ATT_HEAD
    return pl.pallas_call(
        _dsa_prep_kernel,
        grid=(t // tm,),
        in_specs=[pl.BlockSpec((tm, 1), lambda i: (i, 0)), vec, vec,
                  blk(ATT_WIDTH, COL_Q), blk(IDX_HEADS * IDX_HEAD, COL_QI),
                  blk(kvw, COL_KATT), blk(kvw, COL_VATT), blk(LANES, COL_KIWI)],
        out_specs=[out(ATT_WIDTH), out(IDX_HEADS * IDX_HEAD), out(kvw), out(kvw), out(LANES), out(LANES)],
        out_shape=[jax.ShapeDtypeStruct((t, ATT_WIDTH), bf16),
                   jax.ShapeDtypeStruct((t, IDX_HEADS * IDX_HEAD), bf16),
                   jax.ShapeDtypeStruct((t, kvw), bf16),
                   jax.ShapeDtypeStruct((t, kvw), bf16),
                   jax.ShapeDtypeStruct((t, LANES), bf16),
                   jax.ShapeDtypeStruct((t, LANES), f32)],
        compiler_params=_params("parallel"),
        name="dsa_prep",
    )(pos, freq_att, freq_idx, p, p, p, p, p)


def _dsa_kernel(q_ref, qi_ref, wi_ref, k_ref, v_ref, ki2_ref, o_ref, key_ref, *, n_sel):
    tq = q_ref.shape[0]
    s_len = k_ref.shape[0]
    q0 = pl.program_id(1) * tq
    lane = lax.broadcasted_iota(jnp.int32, (1, LANES), 1)
    first_half = lane < IDX_HEAD
    ki2 = ki2_ref[...]
    wi = wi_ref[...]

    score = jnp.zeros((tq, s_len), f32)
    for h in range(IDX_HEADS):
        pair = h // 2
        m = first_half if h % 2 == 0 else jnp.logical_not(first_half)
        qm = jnp.where(m, qi_ref[:, pair * LANES:(pair + 1) * LANES], jnp.zeros((), bf16))
        rel = jnp.maximum(_nt(qm, ki2), 0.0)
        w_h = jnp.sum(jnp.where(lane == h, wi, 0.0), axis=-1, keepdims=True)
        score = score + rel * w_h

    q_pos = q0 + lax.broadcasted_iota(jnp.int32, (tq, 1), 0)
    k_pos = lax.broadcasted_iota(jnp.int32, (1, s_len), 1)
    causal = k_pos <= q_pos
    score = jnp.where(causal, score, -1e30) + 0.0
    bits = pltpu.bitcast(score, jnp.int32)
    key_ref[...] = jnp.where(bits >= 0, bits, bits ^ jnp.int32(0x7FFFFFFF))

    int_min = jnp.int32(-2 ** 31)

    def search(i, t_u):
        cand = t_u | lax.shift_right_logical(int_min, i)
        ge = key_ref[...] >= (cand ^ int_min)
        cnt = jnp.sum(jnp.where(ge, 1.0, 0.0), axis=-1, keepdims=True)
        return jnp.where(cnt >= n_sel, cand, t_u)

    t_u = lax.fori_loop(0, 32, search, jnp.zeros((tq, 1), jnp.int32))
    selected = (key_ref[...] >= (t_u ^ int_min)) & causal

    group = ATT_HEADS // ATT_KV_HEADS
    sel_g = jnp.concatenate([selected] * group, axis=0)
    for kv in range(ATT_KV_HEADS):
        k_h = k_ref[:, kv * LANES:(kv + 1) * LANES]
        v_h = v_ref[:, kv * LANES:(kv + 1) * LANES]
        qg = jnp.concatenate([q_ref[:, (kv * group + g) * LANES:(kv * group + g + 1) * LANES]
                              for g in range(group)], axis=0)
        s = _nt(qg, k_h) * (ATT_HEAD ** -0.5)
        s = jnp.where(sel_g, s, -jnp.inf)
        s_max = jnp.max(s, axis=-1, keepdims=True)
        e = jnp.exp(s - s_max)
        denom = jnp.sum(e, axis=-1, keepdims=True)
        o = _mm(e.astype(bf16), v_h) / denom
        for g in range(group):
            hq = kv * group + g
            o_ref[:, hq * LANES:(hq + 1) * LANES] = o[g * tq:(g + 1) * tq].astype(o_ref.dtype)


def _dsa(q, qi, wi, k, v, ki2, batch, seq, tq):
    nq = seq // tq
    n_sel = min(TOPK_MAX, seq // 4)
    qblk = lambda width: pl.BlockSpec((tq, width), lambda b, i: (b * nq + i, 0))
    kblk = lambda width: pl.BlockSpec((seq, width), lambda b, i: (b, 0))
    kvw = ATT_KV_HEADS * ATT_HEAD
    return pl.pallas_call(
        functools.partial(_dsa_kernel, n_sel=n_sel),
        grid=(batch, nq),
        in_specs=[qblk(ATT_WIDTH), qblk(IDX_HEADS * IDX_HEAD), qblk(LANES),
                  kblk(kvw), kblk(kvw), kblk(LANES)],
        out_specs=qblk(ATT_WIDTH),
        out_shape=jax.ShapeDtypeStruct((batch * seq, ATT_WIDTH), bf16),
        scratch_shapes=[pltpu.VMEM((tq, seq), jnp.int32)],
        compiler_params=_params("parallel", "arbitrary"),
        name="dsa_attention",
    )(q, qi, wi, k, v, ki2)


def _mix_out_kernel(x_ref, ya_ref, yb_ref, wa_ref, wb_ref, o_ref):
    o_ref[...] = x_ref[...] + _mm(ya_ref[...], wa_ref[...]) + _mm(yb_ref[...], wb_ref[...])


def _mix_out(x, ya, yb, w, tm, tn):
    t, d = x.shape
    ka = ya.shape[1]
    kb = yb.shape[1]
    return pl.pallas_call(
        _mix_out_kernel,
        grid=(t // tm, d // tn),
        in_specs=[pl.BlockSpec((tm, tn), lambda i, j: (i, j)),
                  pl.BlockSpec((tm, ka), lambda i, j: (i, 0)),
                  pl.BlockSpec((tm, kb), lambda i, j: (i, 0)),
                  pl.BlockSpec((ka, tn), lambda i, j: (0, j)),
                  pl.BlockSpec((kb, tn), lambda i, j: (ka // kb, j))],
        out_specs=pl.BlockSpec((tm, tn), lambda i, j: (i, j)),
        out_shape=jax.ShapeDtypeStruct((t, d), f32),
        compiler_params=_params("parallel", "arbitrary"),
        name="mix_out",
    )(x, ya, yb, w, w)


def _cross_kernel(h_ref, g_ref, kv_ref, wq_ref, wo_ref, o_ref):
    h = h_ref[...]
    hn = _rms(h, g_ref[...]).astype(bf16)
    q = _mm(hn, wq_ref[...]).astype(bf16)
    width = CROSS_HEADS * CROSS_HEAD
    outs = []
    for hd in range(CROSS_HEADS):
        sl = slice(hd * CROSS_HEAD, (hd + 1) * CROSS_HEAD)
        k_h = kv_ref[:, sl]
        v_h = kv_ref[:, width + hd * CROSS_HEAD: width + (hd + 1) * CROSS_HEAD]
        s = _nt(q[:, sl], k_h) * (CROSS_HEAD ** -0.5)
        e = jnp.exp(s - jnp.max(s, axis=-1, keepdims=True))
        outs.append((_mm(e.astype(bf16), v_h) / jnp.sum(e, axis=-1, keepdims=True)).astype(bf16))
    o = jnp.concatenate(outs, axis=-1)
    o_ref[...] = h + _mm(o, wo_ref[...])


def _cross(h, g, kv, wq, wo, batch, seq, n_mem, tm):
    d = h.shape[1]
    ns = seq // tm
    width = CROSS_HEADS * CROSS_HEAD
    return pl.pallas_call(
        _cross_kernel,
        grid=(batch, ns),
        in_specs=[pl.BlockSpec((tm, d), lambda b, i: (b * ns + i, 0)),
                  pl.BlockSpec((1, d), lambda b, i: (0, 0)),
                  pl.BlockSpec((n_mem, 2 * width), lambda b, i: (b, 0)),
                  pl.BlockSpec((d, width), lambda b, i: (0, 0)),
                  pl.BlockSpec((width, d), lambda b, i: (0, 0))],
        out_specs=pl.BlockSpec((tm, d), lambda b, i: (b * ns + i, 0)),
        out_shape=jax.ShapeDtypeStruct(h.shape, f32),
        compiler_params=_params("parallel", "arbitrary"),
        name="cross_attention",
    )(h, g, kv, wq, wo)


def _mlp_kernel(h_ref, g_ref, wu_ref, wd_ref, gf_ref, o_ref, hn_ref, acc_ref, *, final_norm):
    j = pl.program_id(1)

    @pl.when(j == 0)
    def _():
        hn_ref[...] = _rms(h_ref[...], g_ref[...]).astype(bf16)
        acc_ref[...] = jnp.zeros_like(acc_ref)

    u = jnp.maximum(_mm(hn_ref[...], wu_ref[...]), 0.0)
    acc_ref[...] += _mm((u * u).astype(bf16), wd_ref[...])

    @pl.when(j == pl.num_programs(1) - 1)
    def _():
        out = h_ref[...] + acc_ref[...]
        o_ref[...] = _rms(out, gf_ref[...]) if final_norm else out


def _mlp(h, g, wu, wd, gf, final_norm, tm, tf):
    t, d = h.shape
    dff = wu.shape[1]
    return pl.pallas_call(
        functools.partial(_mlp_kernel, final_norm=final_norm),
        grid=(t // tm, dff // tf),
        in_specs=[pl.BlockSpec((tm, d), lambda i, j: (i, 0)),
                  pl.BlockSpec((1, d), lambda i, j: (0, 0)),
                  pl.BlockSpec((d, tf), lambda i, j: (0, j)),
                  pl.BlockSpec((tf, d), lambda i, j: (j, 0)),
                  pl.BlockSpec((1, d), lambda i, j: (0, 0))],
        out_specs=pl.BlockSpec((tm, d), lambda i, j: (i, 0)),
        out_shape=jax.ShapeDtypeStruct((t, d), f32),
        scratch_shapes=[pltpu.VMEM((tm, d), bf16), pltpu.VMEM((tm, d), f32)],
        compiler_params=_params("parallel", "arbitrary"),
        name="mlp_final_norm",
    )(h, g, wu, wd, gf)


def _pack_in_proj(w_in, mu):
    d = w_in.shape[0]
    o = 0
    seg = {}
    for name, width in (("r", 1024), ("k", 1024), ("v", 1024), ("xw", 64), ("xa", 64), ("xg", 160),
                        ("q", 1024), ("katt", 256), ("vatt", 256), ("qi", 1024), ("ki", 64), ("wi", 16)):
        seg[name] = (o, o + width)
        o += width
    col = lambda n: w_in[:, seg[n][0]:seg[n][1]]
    zeros = lambda n: jnp.zeros((d, n), w_in.dtype)
    w = jnp.concatenate([col("r"), col("k"), col("v"), col("q"), col("qi"), col("katt"), col("vatt"),
                         col("ki"), col("wi"), zeros(LANES - 80),
                         col("xw"), col("xa"), col("xg"), zeros(LORA_BLOCK - 288)], axis=1)
    mu_lo = jnp.concatenate([mu[3072:3360], jnp.zeros((LORA_BLOCK - 288,), mu.dtype)])
    mus = (mu[0:1024][None], mu[1024:2048][None], mu[2048:3072][None], mu_lo[None])
    return w.astype(bf16), mus


def _rope_freq(head, lanes=LANES):
    half = head // 8
    inv_freq = ROPE_THETA ** (-jnp.arange(half, dtype=f32) / half)
    in_head = jnp.arange(lanes) % head
    return jnp.where(in_head < 2 * half, inv_freq[in_head % half], 0.0).astype(f32)[None]


def kernel(x, mem, positions, norm_mix, w_in, rwkv_mu, w_decay0, w_decay_up, a0, a_up, g_up, k_k, k_a, r_k,
           lnx_w, lnx_b, w_mix_out, norm_cross, norm_mem, w_q_cross, w_kv_cross, w_o_cross, norm_mlp,
           w_up, w_down, norm_final):
    batch, seq, d = x.shape
    n_mem = mem.shape[1]
    t = batch * seq
    depth = w_in.shape[0]
    tm = min(512, t)
    h = x.reshape(t, d)
    pos = positions.reshape(t, 1).astype(f32)
    mem2 = mem.reshape(batch * n_mem, d)
    row = lambda vct: vct.reshape(1, -1)

    for l in range(depth):
        w_in_p, mus = _pack_in_proj(w_in[l], rwkv_mu[l])
        zpad = jnp.zeros((DECAY_LORA, RWKV_WIDTH), f32)
        w_wa = jnp.concatenate([jnp.concatenate([w_decay_up[l], zpad], axis=1),
                                jnp.concatenate([zpad, a_up[l]], axis=1)], axis=0).astype(bf16)
        g_up_p = jnp.concatenate([g_up[l], jnp.zeros((LORA_BLOCK - LANES - GATE_LORA, RWKV_WIDTH), f32)],
                                 axis=0).astype(bf16)

        p = _norm_matmul(h, row(norm_mix[l]), w_in_p, f32, tm, 1024)
        y_rwkv = _rwkv(p, batch, seq, mus, row(w_decay0[l]), row(a0[l]), row(k_k[l]), row(k_a[l]),
                       row(lnx_w[l]), row(lnx_b[l]), row(r_k[l]), w_wa, g_up_p)
        q, qi, k_att, v_att, ki2, wi = _dsa_prep(p, pos, _rope_freq(ATT_HEAD), _rope_freq(IDX_HEAD), tm)
        y_att = _dsa(q, qi, wi, k_att, v_att, ki2, batch, seq, min(128, seq))
        h = _mix_out(h, y_rwkv, y_att, w_mix_out[l].astype(bf16), tm, 1024)

        kv = _norm_matmul(mem2, row(norm_mem[l]), w_kv_cross[l].astype(bf16), bf16, min(512, batch * n_mem), 1024)
        h = _cross(h, row(norm_cross[l]), kv, w_q_cross[l].astype(bf16), w_o_cross[l].astype(bf16),
                   batch, seq, n_mem, tm)
        h = _mlp(h, row(norm_mlp[l]), w_up[l].astype(bf16), w_down[l].astype(bf16), row(norm_final),
                 l == depth - 1, tm, 1024)
    return h.reshape(batch, seq, d)
```

```python
import functools

import jax
import jax.numpy as jnp
from jax import lax
from jax.experimental import pallas as pl
from jax.experimental.pallas import tpu as pltpu

f32 = jnp.float32
bf16 = jnp.bfloat16

D_MODEL = 2048
RWKV_WIDTH = 1024
RWKV_HEAD = 64
DECAY_LORA = 64
AAA_LORA = 64
GATE_LORA = 160
RWKV_LN_EPS = 64e-5
ATT_WIDTH = 1024
ATT_HEAD = 128
ATT_HEADS = 8
ATT_KV_HEADS = 2
IDX_HEADS = 16
IDX_HEAD = 64
TOPK_MAX = 256
ROPE_THETA = 500000.0
CROSS_HEADS = 4
CROSS_HEAD = 128
D_FF = 4 * D_MODEL
NORM_EPS = 1e-5

LANES = 128
VMEM_LIMIT_BYTES = 56 * 1024 * 1024

COL_R, COL_K, COL_V, COL_Q, COL_QI = 0, 1024, 2048, 3072, 4096
COL_KATT, COL_VATT, COL_KIWI, COL_LORA = 5120, 5376, 5632, 5760
LORA_BLOCK = 384
IN_COLS_PACKED = 6144
RWKV_CHUNK = 64
DSA_BUCKETS = 8


def _nt(a, b):
    return lax.dot_general(a, b, (((1,), (1,)), ((), ())), preferred_element_type=f32)


def _mm(a, b):
    return jnp.dot(a, b, preferred_element_type=f32)


def _rms(x, g):
    ms = jnp.mean(x * x, axis=-1, keepdims=True)
    return x * lax.rsqrt(ms + NORM_EPS) * g


def _params(*sem):
    return pltpu.CompilerParams(dimension_semantics=sem, vmem_limit_bytes=VMEM_LIMIT_BYTES)


def _norm_matmul_kernel(x_ref, g_ref, w_ref, o_ref, xn_ref):
    @pl.when(pl.program_id(1) == 0)
    def _():
        xn_ref[...] = _rms(x_ref[...], g_ref[...]).astype(bf16)

    o_ref[...] = _mm(xn_ref[...], w_ref[...]).astype(o_ref.dtype)


def _norm_matmul(x, g, w, out_dtype, tm, tn):
    t, d = x.shape
    n = w.shape[1]
    return pl.pallas_call(
        _norm_matmul_kernel,
        grid=(t // tm, n // tn),
        in_specs=[
            pl.BlockSpec((tm, d), lambda i, j: (i, 0)),
            pl.BlockSpec((1, d), lambda i, j: (0, 0)),
            pl.BlockSpec((d, tn), lambda i, j: (0, j)),
        ],
        out_specs=pl.BlockSpec((tm, tn), lambda i, j: (i, j)),
        out_shape=jax.ShapeDtypeStruct((t, n), out_dtype),
        scratch_shapes=[pltpu.VMEM((tm, d), bf16)],
        compiler_params=_params("parallel", "arbitrary"),
        name="norm_matmul",
    )(x, g, w)


def _seg_sum(x, first_half):
    s_a = jnp.sum(jnp.where(first_half, x, 0.0), axis=-1, keepdims=True)
    s_b = jnp.sum(jnp.where(first_half, 0.0, x), axis=-1, keepdims=True)
    return jnp.where(first_half, s_a, s_b)


def _rwkv_kernel(r_ref, k_ref, v_ref, lo_ref, mur_ref, muk_ref, muv_ref, mulo_ref,
                 w0_ref, a0_ref, kkw_ref, kaw_ref, lnw_ref, lnb_ref, rk_ref, wwa_ref, gup_ref,
                 y_ref, st_ref, pr_ref, pk_ref, pv_ref, plo_ref):
    C = r_ref.shape[0]
    n_pairs = RWKV_WIDTH // LANES

    @pl.when(pl.program_id(1) == 0)
    def _():
        st_ref[...] = jnp.zeros_like(st_ref)
        pr_ref[...] = jnp.zeros_like(pr_ref)
        pk_ref[...] = jnp.zeros_like(pk_ref)
        pv_ref[...] = jnp.zeros_like(pv_ref)
        plo_ref[...] = jnp.zeros_like(plo_ref)

    row = lax.broadcasted_iota(jnp.int32, (C, 1), 0)

    def shifted(x_ref, prev_ref, mu_ref):
        x = x_ref[...]
        prev = jnp.where(row == 0, prev_ref[...], pltpu.roll(x, 1, 0))
        prev_ref[...] = x[C - 1:C, :]
        return x + (prev - x) * mu_ref[...]

    r = shifted(r_ref, pr_ref, mur_ref)
    k = shifted(k_ref, pk_ref, muk_ref)
    v = shifted(v_ref, pv_ref, muv_ref)
    lo = shifted(lo_ref, plo_ref, mulo_ref)

    lane = lax.broadcasted_iota(jnp.int32, (1, LANES), 1)
    first_half = lane < RWKV_HEAD

    wa_in = lo[:, :LANES]
    wa_in = jnp.where(first_half, jnp.tanh(wa_in), wa_in).astype(bf16)
    wa = _mm(wa_in, wwa_ref[...])
    zw = -(w0_ref[...] + wa[:, :RWKV_WIDTH])
    softplus = jnp.maximum(zw, 0.0) + jnp.log(1.0 + jnp.exp(-jnp.abs(zw)))
    logdecay = -jnp.exp(-softplus - 0.5)
    a = 1.0 / (1.0 + jnp.exp(-(a0_ref[...] + wa[:, RWKV_WIDTH:])))
    xg = lo[:, LANES:]
    gate = _mm((1.0 / (1.0 + jnp.exp(-xg))).astype(bf16), gup_ref[...])

    ti = lax.broadcasted_iota(jnp.int32, (C, C), 0)
    tj = lax.broadcasted_iota(jnp.int32, (C, C), 1)
    tri_incl = ti >= tj
    tri_strict = ti > tj
    tri_b = jnp.where(tri_incl, 1.0, 0.0).astype(bf16)
    ld_hi = logdecay.astype(bf16)
    ld_r1 = logdecay - ld_hi.astype(f32)
    ld_mid = ld_r1.astype(bf16)
    ld_lo = (ld_r1 - ld_mid.astype(f32)).astype(bf16)
    cum = _mm(tri_b, ld_hi) + _mm(tri_b, ld_mid) + _mm(tri_b, ld_lo)

    kk = k * kkw_ref[...]
    k2 = k * (1.0 + (a - 1.0) * kaw_ref[...])
    rkk = r * k2 * rk_ref[...]
    eye = jnp.where(ti == tj, 1.0, 0.0)
    vi = lax.broadcasted_iota(jnp.int32, (LANES, LANES), 0)
    vj = lax.broadcasted_iota(jnp.int32, (LANES, LANES), 1)
    same_head = (vi < RWKV_HEAD) == (vj < RWKV_HEAD)

    pairs = range(n_pairs)
    heads = [(p, h) for p in pairs for h in range(2)]
    lanes_of = lambda p: slice(p * LANES, (p + 1) * LANES)
    half_mask = (first_half, jnp.logical_not(first_half))

    ecum = jnp.exp(cum)
    encum = jnp.exp(-cum)
    ecum_prev = jnp.exp(cum - logdecay)
    a_t, b_t, k_t, r_t, g_end, s0 = [], [], [], [], [], []
    for p in pairs:
        sl = lanes_of(p)
        kkp = kk[:, sl]
        kkn = kkp / jnp.maximum(jnp.sqrt(_seg_sum(kkp * kkp, first_half)), 1e-12)
        a_t.append(-kkn * ecum_prev[:, sl])
        b_t.append(kkn * a[:, sl] * encum[:, sl])
        k_t.append(k2[:, sl] * encum[:, sl])
        r_t.append(r[:, sl] * ecum[:, sl])
        g_end.append(ecum[C - 1:C, sl])
        s0.append(st_ref[p])
    b_b = [x.astype(bf16) for x in b_t]
    k_b = [x.astype(bf16) for x in k_t]
    v_b = [v[:, lanes_of(p)].astype(bf16) for p in pairs]
    ar0 = [_nt(jnp.concatenate([a_t[p], r_t[p]], axis=0).astype(bf16), s0[p].astype(bf16)) for p in pairs]
    a_h = [jnp.where(half_mask[h], a_t[p], 0.0).astype(bf16) for p, h in heads]
    r_h = [jnp.where(half_mask[h], r_t[p], 0.0).astype(bf16) for p, h in heads]
    l_b = [jnp.where(tri_strict, _nt(a_h[i], b_b[p]), 0.0) for i, (p, h) in enumerate(heads)]
    l_k = [jnp.where(tri_strict, _nt(a_h[i], k_b[p]), 0.0).astype(bf16) for i, (p, h) in enumerate(heads)]
    m_b = [jnp.where(tri_incl, _nt(r_h[i], b_b[p]), 0.0).astype(bf16) for i, (p, h) in enumerate(heads)]
    m_k = [jnp.where(tri_incl, _nt(r_h[i], k_b[p]), 0.0).astype(bf16) for i, (p, h) in enumerate(heads)]
    w_h = [(ar0[p][:C] + _mm(l_k[i], v_b[p])).astype(bf16) for i, (p, h) in enumerate(heads)]
    inv = [eye + x for x in l_b]
    l_pow = l_b
    for _ in range(C.bit_length() - 2):
        l_pow_b = [x.astype(bf16) for x in l_pow]
        l_pow = [_mm(x, x) for x in l_pow_b]
        inv = [x + _mm(x.astype(bf16), y.astype(bf16)) for x, y in zip(inv, l_pow)]
    u_h = [_mm(x.astype(bf16), y) for x, y in zip(inv, w_h)]
    u = [jnp.where(first_half, u_h[2 * p], u_h[2 * p + 1]) for p in pairs]
    u_b = [x.astype(bf16) for x in u]
    y_h = [_mm(m_b[i], u_b[p]) + _mm(m_k[i], v_b[p]) for i, (p, h) in enumerate(heads)]
    upd = [_mm(jnp.concatenate([u[p], v[:, lanes_of(p)]], axis=0).T.astype(bf16),
               (jnp.concatenate([b_t[p], k_t[p]], axis=0) * g_end[p]).astype(bf16)) for p in pairs]
    for p in pairs:
        sl = lanes_of(p)
        st_ref[p] = s0[p] * g_end[p] + jnp.where(same_head, upd[p], 0.0)
        y = ar0[p][C:] + jnp.where(first_half, y_h[2 * p], y_h[2 * p + 1])
        mean = _seg_sum(y, first_half) * (1.0 / RWKV_HEAD)
        yc = y - mean
        var = _seg_sum(yc * yc, first_half) * (1.0 / RWKV_HEAD)
        yn = yc * lax.rsqrt(var + RWKV_LN_EPS) * lnw_ref[:, sl] + lnb_ref[:, sl]
        bonus = _seg_sum(rkk[:, sl], first_half) * v[:, sl]
        y_ref[:, sl] = ((yn + bonus) * gate[:, sl]).astype(y_ref.dtype)


def _rwkv(p, batch, seq, mu, w0, a0, k_k, k_a, lnx_w, lnx_b, r_k, w_wa, g_up_p):
    C = RWKV_CHUNK
    nc = seq // C
    W = RWKV_WIDTH
    row_block = lambda width, col: pl.BlockSpec((C, width), lambda b, c: (b * nc + c, col // width))
    vec = lambda width: pl.BlockSpec((1, width), lambda b, c: (0, 0))
    full = lambda arr: pl.BlockSpec(arr.shape, lambda b, c: (0, 0))
    mu_r, mu_k, mu_v, mu_lo = mu
    return pl.pallas_call(
        _rwkv_kernel,
        grid=(batch, nc),
        in_specs=[row_block(W, COL_R), row_block(W, COL_K), row_block(W, COL_V),
                  row_block(LORA_BLOCK, COL_LORA),
                  vec(W), vec(W), vec(W), vec(LORA_BLOCK),
                  vec(W), vec(W), vec(W), vec(W), vec(W), vec(W), vec(W),
                  full(w_wa), full(g_up_p)],
        out_specs=pl.BlockSpec((C, W), lambda b, c: (b * nc + c, 0)),
        out_shape=jax.ShapeDtypeStruct((batch * seq, W), bf16),
        scratch_shapes=[pltpu.VMEM((W // LANES, LANES, LANES), f32),
                        pltpu.VMEM((1, W), f32), pltpu.VMEM((1, W), f32), pltpu.VMEM((1, W), f32),
                        pltpu.VMEM((1, LORA_BLOCK), f32)],
        compiler_params=_params("parallel", "arbitrary"),
        name="rwkv7_chunked",
    )(p, p, p, p, mu_r, mu_k, mu_v, mu_lo, w0, a0, k_k, k_a, lnx_w, lnx_b, r_k, w_wa, g_up_p)


def _rope(x, cos, sin_lo, sin_hi, half):
    n = x.shape[-1]
    return x * cos + pltpu.roll(x, n - half, 1) * sin_lo + pltpu.roll(x, half, 1) * sin_hi


def _dsa_prep_kernel(pos_ref, fa_ref, fi_ref, q_ref, qi_ref, ka_ref, va_ref, kiwi_ref,
                     qo_ref, qio_ref, ko_ref, vo_ref, ki2_ref, wi_ref):
    pos = pos_ref[...]
    lane = lax.broadcasted_iota(jnp.int32, (1, LANES), 1)

    def tables(freq, head, half):
        ang = pos * freq
        cos, sin = jnp.cos(ang), jnp.sin(ang)
        in_head = lane % head
        sin_lo = jnp.where(in_head < half, -sin, 0.0)
        sin_hi = jnp.where((in_head >= half) & (in_head < 2 * half), sin, 0.0)
        return cos, sin_lo, sin_hi

    ca, sla, sha = tables(fa_ref[...], ATT_HEAD, ATT_HEAD // 8)
    ci, sli, shi = tables(fi_ref[...], IDX_HEAD, IDX_HEAD // 8)
    for h in range(ATT_HEADS):
        sl = slice(h * LANES, (h + 1) * LANES)
        qo_ref[:, sl] = _rope(q_ref[:, sl], ca, sla, sha, ATT_HEAD // 8).astype(bf16)
        qio_ref[:, sl] = _rope(qi_ref[:, sl], ci, sli, shi, IDX_HEAD // 8).astype(bf16)
    for h in range(ATT_KV_HEADS):
        sl = slice(h * LANES, (h + 1) * LANES)
        ko_ref[:, sl] = _rope(ka_ref[:, sl], ca, sla, sha, ATT_HEAD // 8).astype(bf16)
    vo_ref[...] = va_ref[...].astype(bf16)
    kiwi = kiwi_ref[...]
    ki_only = jnp.where(lane < IDX_HEAD, kiwi, 0.0)
    ki = _rope(ki_only, ci, sli, shi, IDX_HEAD // 8)
    ki2_ref[...] = (ki + pltpu.roll(ki, IDX_HEAD, 1)).astype(bf16)
    wi = pltpu.roll(kiwi, LANES - IDX_HEAD, 1)
    wi_ref[...] = jnp.where(lane < IDX_HEADS, wi, 0.0) * (IDX_HEADS ** -0.5 * IDX_HEAD ** -0.5)


def _dsa_prep(p, pos, freq_att, freq_idx, tm):
    t = p.shape[0]
    blk = lambda width, col: pl.BlockSpec((tm, width), lambda i: (i, col // width))
    out = lambda width: pl.BlockSpec((tm, width), lambda i: (i, 0))
    vec = pl.BlockSpec((1, LANES), lambda i: (0, 0))
    kvw = ATT_KV_HEADS * ATT_HEAD
    return pl.pallas_call(
        _dsa_prep_kernel,
        grid=(t // tm,),
        in_specs=[pl.BlockSpec((tm, 1), lambda i: (i, 0)), vec, vec,
                  blk(ATT_WIDTH, COL_Q), blk(IDX_HEADS * IDX_HEAD, COL_QI),
                  blk(kvw, COL_KATT), blk(kvw, COL_VATT), blk(LANES, COL_KIWI)],
        out_specs=[out(ATT_WIDTH), out(IDX_HEADS * IDX_HEAD), out(kvw), out(kvw), out(LANES), out(LANES)],
        out_shape=[jax.ShapeDtypeStruct((t, ATT_WIDTH), bf16),
                   jax.ShapeDtypeStruct((t, IDX_HEADS * IDX_HEAD), bf16),
                   jax.ShapeDtypeStruct((t, kvw), bf16),
                   jax.ShapeDtypeStruct((t, kvw), bf16),
                   jax.ShapeDtypeStruct((t, LANES), bf16),
                   jax.ShapeDtypeStruct((t, LANES), f32)],
        compiler_params=_params("parallel"),
        name="dsa_prep",
    )(pos, freq_att, freq_idx, p, p, p, p, p)


def _dsa_kernel(q_ref, qi_ref, wi_ref, k_ref, v_ref, ki2_ref, o_ref, key_ref, *, n_sel, bucket_len):
    tq = q_ref.shape[0]
    seq = k_ref.shape[0]
    bucket = ((pl.program_id(1) + 1) * tq - 1) // bucket_len
    for j in range(seq // bucket_len):
        pl.when(bucket == j)(functools.partial(
            _dsa_body, q_ref, qi_ref, wi_ref, k_ref, v_ref, ki2_ref, o_ref, key_ref,
            n_sel=n_sel, s_len=(j + 1) * bucket_len))


def _dsa_body(q_ref, qi_ref, wi_ref, k_ref, v_ref, ki2_ref, o_ref, key_ref, *, n_sel, s_len):
    tq = q_ref.shape[0]
    q0 = pl.program_id(1) * tq
    lane = lax.broadcasted_iota(jnp.int32, (1, LANES), 1)
    first_half = lane < IDX_HEAD
    ki2 = ki2_ref[0:s_len, :]
    wi = wi_ref[...]

    score = jnp.zeros((tq, s_len), f32)
    for h in range(IDX_HEADS):
        pair = h // 2
        m = first_half if h % 2 == 0 else jnp.logical_not(first_half)
        qm = jnp.where(m, qi_ref[:, pair * LANES:(pair + 1) * LANES], jnp.zeros((), bf16))
        rel = jnp.maximum(_nt(qm, ki2), 0.0)
        w_h = jnp.sum(jnp.where(lane == h, wi, 0.0), axis=-1, keepdims=True)
        score = score + rel * w_h

    q_pos = q0 + lax.broadcasted_iota(jnp.int32, (tq, 1), 0)
    k_pos = lax.broadcasted_iota(jnp.int32, (1, s_len), 1)
    causal = k_pos <= q_pos
    score = jnp.where(causal, score, -1e30) + 0.0
    bits = pltpu.bitcast(score, jnp.int32)
    key_ref[:, 0:s_len] = jnp.where(bits >= 0, bits, bits ^ jnp.int32(0x7FFFFFFF))

    int_min = jnp.int32(-2 ** 31)

    def search(i, t_u):
        cand = t_u | lax.shift_right_logical(int_min, i)
        ge = key_ref[:, 0:s_len] >= (cand ^ int_min)
        cnt = jnp.sum(jnp.where(ge, 1.0, 0.0), axis=-1, keepdims=True)
        return jnp.where(cnt >= n_sel, cand, t_u)

    t_u = lax.fori_loop(0, 32, search, jnp.zeros((tq, 1), jnp.int32))
    selected = (key_ref[:, 0:s_len] >= (t_u ^ int_min)) & causal

    group = ATT_HEADS // ATT_KV_HEADS
    sel_g = jnp.concatenate([selected] * group, axis=0)
    for kv in range(ATT_KV_HEADS):
        k_h = k_ref[0:s_len, kv * LANES:(kv + 1) * LANES]
        v_h = v_ref[0:s_len, kv * LANES:(kv + 1) * LANES]
        qg = jnp.concatenate([q_ref[:, (kv * group + g) * LANES:(kv * group + g + 1) * LANES]
                              for g in range(group)], axis=0)
        s = _nt(qg, k_h) * (ATT_HEAD ** -0.5)
        s = jnp.where(sel_g, s, -jnp.inf)
        s_max = jnp.max(s, axis=-1, keepdims=True)
        e = jnp.exp(s - s_max)
        denom = jnp.sum(e, axis=-1, keepdims=True)
        o = _mm(e.astype(bf16), v_h) / denom
        for g in range(group):
            hq = kv * group + g
            o_ref[:, hq * LANES:(hq + 1) * LANES] = o[g * tq:(g + 1) * tq].astype(o_ref.dtype)


def _dsa(q, qi, wi, k, v, ki2, batch, seq, tq):
    nq = seq // tq
    n_sel = min(TOPK_MAX, seq // 4)
    qblk = lambda width: pl.BlockSpec((tq, width), lambda b, i: (b * nq + i, 0))
    kblk = lambda width: pl.BlockSpec((seq, width), lambda b, i: (b, 0))
    kvw = ATT_KV_HEADS * ATT_HEAD
    bucket_len = max(tq, seq // DSA_BUCKETS)
    return pl.pallas_call(
        functools.partial(_dsa_kernel, n_sel=n_sel, bucket_len=bucket_len),
        grid=(batch, nq),
        in_specs=[qblk(ATT_WIDTH), qblk(IDX_HEADS * IDX_HEAD), qblk(LANES),
                  kblk(kvw), kblk(kvw), kblk(LANES)],
        out_specs=qblk(ATT_WIDTH),
        out_shape=jax.ShapeDtypeStruct((batch * seq, ATT_WIDTH), bf16),
        scratch_shapes=[pltpu.VMEM((tq, seq), jnp.int32)],
        compiler_params=_params("parallel", "arbitrary"),
        name="dsa_attention",
    )(q, qi, wi, k, v, ki2)


def _mix_out_kernel(x_ref, ya_ref, yb_ref, wa_ref, wb_ref, o_ref):
    o_ref[...] = x_ref[...] + _mm(ya_ref[...], wa_ref[...]) + _mm(yb_ref[...], wb_ref[...])


def _mix_out(x, ya, yb, w, tm, tn):
    t, d = x.shape
    ka = ya.shape[1]
    kb = yb.shape[1]
    return pl.pallas_call(
        _mix_out_kernel,
        grid=(t // tm, d // tn),
        in_specs=[pl.BlockSpec((tm, tn), lambda i, j: (i, j)),
                  pl.BlockSpec((tm, ka), lambda i, j: (i, 0)),
                  pl.BlockSpec((tm, kb), lambda i, j: (i, 0)),
                  pl.BlockSpec((ka, tn), lambda i, j: (0, j)),
                  pl.BlockSpec((kb, tn), lambda i, j: (ka // kb, j))],
        out_specs=pl.BlockSpec((tm, tn), lambda i, j: (i, j)),
        out_shape=jax.ShapeDtypeStruct((t, d), f32),
        compiler_params=_params("parallel", "arbitrary"),
        name="mix_out",
    )(x, ya, yb, w, w)


def _cross_kernel(h_ref, g_ref, kv_ref, wq_ref, wo_ref, o_ref):
    h = h_ref[...]
    hn = _rms(h, g_ref[...]).astype(bf16)
    q = _mm(hn, wq_ref[...]).astype(bf16)
    width = CROSS_HEADS * CROSS_HEAD
    outs = []
    for hd in range(CROSS_HEADS):
        sl = slice(hd * CROSS_HEAD, (hd + 1) * CROSS_HEAD)
        k_h = kv_ref[:, sl]
        v_h = kv_ref[:, width + hd * CROSS_HEAD: width + (hd + 1) * CROSS_HEAD]
        s = _nt(q[:, sl], k_h) * (CROSS_HEAD ** -0.5)
        e = jnp.exp(s - jnp.max(s, axis=-1, keepdims=True))
        outs.append((_mm(e.astype(bf16), v_h) / jnp.sum(e, axis=-1, keepdims=True)).astype(bf16))
    o = jnp.concatenate(outs, axis=-1)
    o_ref[...] = h + _mm(o, wo_ref[...])


def _cross(h, g, kv, wq, wo, batch, seq, n_mem, tm):
    d = h.shape[1]
    ns = seq // tm
    width = CROSS_HEADS * CROSS_HEAD
    return pl.pallas_call(
        _cross_kernel,
        grid=(batch, ns),
        in_specs=[pl.BlockSpec((tm, d), lambda b, i: (b * ns + i, 0)),
                  pl.BlockSpec((1, d), lambda b, i: (0, 0)),
                  pl.BlockSpec((n_mem, 2 * width), lambda b, i: (b, 0)),
                  pl.BlockSpec((d, width), lambda b, i: (0, 0)),
                  pl.BlockSpec((width, d), lambda b, i: (0, 0))],
        out_specs=pl.BlockSpec((tm, d), lambda b, i: (b * ns + i, 0)),
        out_shape=jax.ShapeDtypeStruct(h.shape, f32),
        compiler_params=_params("parallel", "arbitrary"),
        name="cross_attention",
    )(h, g, kv, wq, wo)


def _mlp_kernel(h_ref, g_ref, wu_ref, wd_ref, gf_ref, o_ref, hn_ref, acc_ref, *, final_norm):
    j = pl.program_id(1)

    @pl.when(j == 0)
    def _():
        hn_ref[...] = _rms(h_ref[...], g_ref[...]).astype(bf16)
        acc_ref[...] = jnp.zeros_like(acc_ref)

    u = jnp.maximum(_mm(hn_ref[...], wu_ref[...]), 0.0)
    acc_ref[...] += _mm((u * u).astype(bf16), wd_ref[...])

    @pl.when(j == pl.num_programs(1) - 1)
    def _():
        out = h_ref[...] + acc_ref[...]
        o_ref[...] = _rms(out, gf_ref[...]) if final_norm else out


def _mlp(h, g, wu, wd, gf, final_norm, tm, tf):
    t, d = h.shape
    dff = wu.shape[1]
    return pl.pallas_call(
        functools.partial(_mlp_kernel, final_norm=final_norm),
        grid=(t // tm, dff // tf),
        in_specs=[pl.BlockSpec((tm, d), lambda i, j: (i, 0)),
                  pl.BlockSpec((1, d), lambda i, j: (0, 0)),
                  pl.BlockSpec((d, tf), lambda i, j: (0, j)),
                  pl.BlockSpec((tf, d), lambda i, j: (j, 0)),
                  pl.BlockSpec((1, d), lambda i, j: (0, 0))],
        out_specs=pl.BlockSpec((tm, d), lambda i, j: (i, 0)),
        out_shape=jax.ShapeDtypeStruct((t, d), f32),
        scratch_shapes=[pltpu.VMEM((tm, d), bf16), pltpu.VMEM((tm, d), f32)],
        compiler_params=_params("parallel", "arbitrary"),
        name="mlp_final_norm",
    )(h, g, wu, wd, gf)


def _pack_in_proj(w_in, mu):
    d = w_in.shape[0]
    o = 0
    seg = {}
    for name, width in (("r", 1024), ("k", 1024), ("v", 1024), ("xw", 64), ("xa", 64), ("xg", 160),
                        ("q", 1024), ("katt", 256), ("vatt", 256), ("qi", 1024), ("ki", 64), ("wi", 16)):
        seg[name] = (o, o + width)
        o += width
    col = lambda n: w_in[:, seg[n][0]:seg[n][1]]
    zeros = lambda n: jnp.zeros((d, n), w_in.dtype)
    w = jnp.concatenate([col("r"), col("k"), col("v"), col("q"), col("qi"), col("katt"), col("vatt"),
                         col("ki"), col("wi"), zeros(LANES - 80),
                         col("xw"), col("xa"), col("xg"), zeros(LORA_BLOCK - 288)], axis=1)
    mu_lo = jnp.concatenate([mu[3072:3360], jnp.zeros((LORA_BLOCK - 288,), mu.dtype)])
    mus = (mu[0:1024][None], mu[1024:2048][None], mu[2048:3072][None], mu_lo[None])
    return w.astype(bf16), mus


def _rope_freq(head, lanes=LANES):
    half = head // 8
    inv_freq = ROPE_THETA ** (-jnp.arange(half, dtype=f32) / half)
    in_head = jnp.arange(lanes) % head
    return jnp.where(in_head < 2 * half, inv_freq[in_head % half], 0.0).astype(f32)[None]


def kernel(x, mem, positions, norm_mix, w_in, rwkv_mu, w_decay0, w_decay_up, a0, a_up, g_up, k_k, k_a, r_k,
           lnx_w, lnx_b, w_mix_out, norm_cross, norm_mem, w_q_cross, w_kv_cross, w_o_cross, norm_mlp,
           w_up, w_down, norm_final):
    batch, seq, d = x.shape
    n_mem = mem.shape[1]
    t = batch * seq
    depth = w_in.shape[0]
    tm = min(512, t)
    h = x.reshape(t, d)
    pos = positions.reshape(t, 1).astype(f32)
    mem2 = mem.reshape(batch * n_mem, d)
    row = lambda vct: vct.reshape(1, -1)

    for l in range(depth):
        w_in_p, mus = _pack_in_proj(w_in[l], rwkv_mu[l])
        zpad = jnp.zeros((DECAY_LORA, RWKV_WIDTH), f32)
        w_wa = jnp.concatenate([jnp.concatenate([w_decay_up[l], zpad], axis=1),
                                jnp.concatenate([zpad, a_up[l]], axis=1)], axis=0).astype(bf16)
        g_up_p = jnp.concatenate([g_up[l], jnp.zeros((LORA_BLOCK - LANES - GATE_LORA, RWKV_WIDTH), f32)],
                                 axis=0).astype(bf16)

        p = _norm_matmul(h, row(norm_mix[l]), w_in_p, f32, min(1024, t), 1024)
        y_rwkv = _rwkv(p, batch, seq, mus, row(w_decay0[l]), row(a0[l]), row(k_k[l]), row(k_a[l]),
                       row(lnx_w[l]), row(lnx_b[l]), row(r_k[l]), w_wa, g_up_p)
        q, qi, k_att, v_att, ki2, wi = _dsa_prep(p, pos, _rope_freq(ATT_HEAD), _rope_freq(IDX_HEAD), tm)
        y_att = _dsa(q, qi, wi, k_att, v_att, ki2, batch, seq, min(128, seq))
        h = _mix_out(h, y_rwkv, y_att, w_mix_out[l].astype(bf16), tm, d)

        kv = _norm_matmul(mem2, row(norm_mem[l]), w_kv_cross[l].astype(bf16), bf16, min(512, batch * n_mem), 1024)
        h = _cross(h, row(norm_cross[l]), kv, w_q_cross[l].astype(bf16), w_o_cross[l].astype(bf16),
                   batch, seq, n_mem, tm)
        h = _mlp(h, row(norm_mlp[l]), w_up[l].astype(bf16), w_down[l].astype(bf16), row(norm_final),
                 l == depth - 1, tm, 1024)
    return h.reshape(batch, seq, d)
```

```python
import functools

import jax
import jax.numpy as jnp
from jax import lax
from jax.experimental import pallas as pl
from jax.experimental.pallas import tpu as pltpu

f32 = jnp.float32
bf16 = jnp.bfloat16

D_MODEL = 2048
RWKV_WIDTH = 1024
RWKV_HEAD = 64
DECAY_LORA = 64
AAA_LORA = 64
GATE_LORA = 160
RWKV_LN_EPS = 64e-5
ATT_WIDTH = 1024
ATT_HEAD = 128
ATT_HEADS = 8
ATT_KV_HEADS = 2
IDX_HEADS = 16
IDX_HEAD = 64
TOPK_MAX = 256
ROPE_THETA = 500000.0
CROSS_HEADS = 4
CROSS_HEAD = 128
D_FF = 4 * D_MODEL
NORM_EPS = 1e-5

LANES = 128
VMEM_LIMIT_BYTES = 56 * 1024 * 1024

COL_R, COL_K, COL_V, COL_Q, COL_QI = 0, 1024, 2048, 3072, 4096
COL_KATT, COL_VATT, COL_KIWI, COL_LORA = 5120, 5376, 5632, 5760
LORA_BLOCK = 384
IN_COLS_PACKED = 6144
RWKV_CHUNK = 64
_EXP_NEG_HALF = 0.6065306597126334
RWKV_BATCH_GROUP = 2
DSA_BUCKETS = 8


def _nt(a, b):
    return lax.dot_general(a, b, (((1,), (1,)), ((), ())), preferred_element_type=f32)


def _mm(a, b):
    return jnp.dot(a, b, preferred_element_type=f32)


def _rms(x, g):
    ms = jnp.mean(x * x, axis=-1, keepdims=True)
    return x * lax.rsqrt(ms + NORM_EPS) * g


def _params(*sem):
    return pltpu.CompilerParams(dimension_semantics=sem, vmem_limit_bytes=VMEM_LIMIT_BYTES)


def _norm_matmul_kernel(x_ref, g_ref, w_ref, o_ref, xn_ref):
    @pl.when(pl.program_id(1) == 0)
    def _():
        xn_ref[...] = _rms(x_ref[...], g_ref[...]).astype(bf16)

    o_ref[...] = _mm(xn_ref[...], w_ref[...]).astype(o_ref.dtype)


def _norm_matmul(x, g, w, out_dtype, tm, tn):
    t, d = x.shape
    n = w.shape[1]
    return pl.pallas_call(
        _norm_matmul_kernel,
        grid=(t // tm, n // tn),
        in_specs=[
            pl.BlockSpec((tm, d), lambda i, j: (i, 0)),
            pl.BlockSpec((1, d), lambda i, j: (0, 0)),
            pl.BlockSpec((d, tn), lambda i, j: (0, j)),
        ],
        out_specs=pl.BlockSpec((tm, tn), lambda i, j: (i, j)),
        out_shape=jax.ShapeDtypeStruct((t, n), out_dtype),
        scratch_shapes=[pltpu.VMEM((tm, d), bf16)],
        compiler_params=_params("parallel", "arbitrary"),
        name="norm_matmul",
    )(x, g, w)


def _seg_sum(x, first_half):
    s_a = jnp.sum(jnp.where(first_half, x, 0.0), axis=-1, keepdims=True)
    s_b = jnp.sum(jnp.where(first_half, 0.0, x), axis=-1, keepdims=True)
    return jnp.where(first_half, s_a, s_b)


def _rwkv_kernel(r_ref, k_ref, v_ref, lo_ref, mur_ref, muk_ref, muv_ref, mulo_ref,
                 w0_ref, a0_ref, kkw_ref, kaw_ref, lnw_ref, lnb_ref, rk_ref, wwa_ref, gup_ref,
                 y_ref, st_ref, pr_ref, pk_ref, pv_ref, plo_ref):
    G, C = r_ref.shape[0], r_ref.shape[1]
    n_pairs = RWKV_WIDTH // LANES

    @pl.when(pl.program_id(1) == 0)
    def _():
        st_ref[...] = jnp.zeros_like(st_ref)
        pr_ref[...] = jnp.zeros_like(pr_ref)
        pk_ref[...] = jnp.zeros_like(pk_ref)
        pv_ref[...] = jnp.zeros_like(pv_ref)
        plo_ref[...] = jnp.zeros_like(plo_ref)

    row = lax.broadcasted_iota(jnp.int32, (C, 1), 0)
    lane = lax.broadcasted_iota(jnp.int32, (1, LANES), 1)
    first_half = lane < RWKV_HEAD
    ti = lax.broadcasted_iota(jnp.int32, (C, C), 0)
    tj = lax.broadcasted_iota(jnp.int32, (C, C), 1)
    tri_b = jnp.where(ti >= tj, 1.0, 0.0).astype(bf16)
    t2 = lax.broadcasted_iota(jnp.int32, (C, 2 * C), 0)
    j2 = lax.broadcasted_iota(jnp.int32, (C, 2 * C), 1) % C
    tri2_strict = t2 > j2
    eye2 = jnp.where(t2 == j2, 1.0, 0.0)
    t4 = lax.broadcasted_iota(jnp.int32, (C, 4 * C), 0)
    j4 = lax.broadcasted_iota(jnp.int32, (C, 4 * C), 1) % C
    tri4_incl = t4 >= j4
    vi = lax.broadcasted_iota(jnp.int32, (LANES, LANES), 0)
    vj = lax.broadcasted_iota(jnp.int32, (LANES, LANES), 1)
    same_head = (vi < RWKV_HEAD) == (vj < RWKV_HEAD)

    def head_rows(x):
        xb = x.astype(bf16)
        zero = jnp.zeros((), bf16)
        return jnp.concatenate([jnp.where(first_half, xb, zero), jnp.where(first_half, zero, xb)], axis=0)

    def shifted(x_ref, prev_ref, mu_ref, g):
        x = x_ref[g]
        prev = jnp.where(row == 0, prev_ref[g], pltpu.roll(x, 1, 0))
        prev_ref[g] = x[C - 1:C, :]
        return x + (prev - x) * mu_ref[...]

    units = [(g, p) for g in range(G) for p in range(n_pairs)]
    lanes_of = lambda p: slice(p * LANES, (p + 1) * LANES)
    v_all, gate_all, rkk_all = [], [], []
    a_t, b_t, k_t, r_t, g_end, s0 = [], [], [], [], [], []
    for g in range(G):
        r = shifted(r_ref, pr_ref, mur_ref, g)
        k = shifted(k_ref, pk_ref, muk_ref, g)
        v = shifted(v_ref, pv_ref, muv_ref, g)
        lo = shifted(lo_ref, plo_ref, mulo_ref, g)
        wa_in = lo[:, :LANES]
        wa_in = jnp.where(first_half, jnp.tanh(wa_in), wa_in).astype(bf16)
        wa = _mm(wa_in, wwa_ref[...])
        logdecay = -_EXP_NEG_HALF / (1.0 + jnp.exp(-(w0_ref[...] + wa[:, :RWKV_WIDTH])))
        a = 1.0 / (1.0 + jnp.exp(-(a0_ref[...] + wa[:, RWKV_WIDTH:])))
        gate_all.append(_mm((1.0 / (1.0 + jnp.exp(-lo[:, LANES:]))).astype(bf16), gup_ref[...]))
        ld_hi = logdecay.astype(bf16)
        ld_r1 = logdecay - ld_hi.astype(f32)
        ld_mid = ld_r1.astype(bf16)
        ld_lo = (ld_r1 - ld_mid.astype(f32)).astype(bf16)
        cum = _mm(tri_b, ld_hi) + _mm(tri_b, ld_mid) + _mm(tri_b, ld_lo)
        kk = k * kkw_ref[...]
        k2 = k * (1.0 + (a - 1.0) * kaw_ref[...])
        rkk_all.append(r * k2 * rk_ref[...])
        v_all.append(v)
        ecum = jnp.exp(cum)
        encum = jnp.exp(-cum)
        ecum_prev = jnp.exp(cum - logdecay)
        for p in range(n_pairs):
            sl = lanes_of(p)
            kkp = kk[:, sl]
            kkn = kkp / jnp.maximum(jnp.sqrt(_seg_sum(kkp * kkp, first_half)), 1e-12)
            a_t.append(-kkn * ecum_prev[:, sl])
            b_t.append(kkn * a[:, sl] * encum[:, sl])
            k_t.append(k2[:, sl] * encum[:, sl])
            r_t.append(r[:, sl] * ecum[:, sl])
            g_end.append(ecum[C - 1:C, sl])
            s0.append(st_ref[g * n_pairs + p])

    n = len(units)
    vp = [v_all[g][:, lanes_of(p)] for g, p in units]
    v_rows = [head_rows(x) for x in vp]
    ar = [jnp.concatenate([a_t[i], r_t[i]], axis=0).astype(bf16) for i in range(n)]
    ar0 = [_nt(ar[i], s0[i].astype(bf16)) for i in range(n)]
    gram = [_nt(ar[i], jnp.concatenate([head_rows(b_t[i]), head_rows(k_t[i])], axis=0)) for i in range(n)]
    l_b = [jnp.where(tri2_strict, x[:C, :2 * C], 0.0) for x in gram]
    l_k = [jnp.where(tri2_strict, x[:C, 2 * C:], 0.0).astype(bf16) for x in gram]
    m_bk = [jnp.where(tri4_incl, x[C:, :], 0.0).astype(bf16) for x in gram]
    w = [ar0[i][:C] + _mm(l_k[i], v_rows[i]) for i in range(n)]
    inv = [eye2 + x for x in l_b]
    l_pow = l_b
    l_pow_rows = [head_rows(x) for x in l_pow]
    for _ in range(C.bit_length() - 2):
        l_pow = [_mm(x.astype(bf16), y) for x, y in zip(l_pow, l_pow_rows)]
        l_pow_rows = [head_rows(x) for x in l_pow]
        inv = [x + _mm(x.astype(bf16), y) for x, y in zip(inv, l_pow_rows)]
    u = [_mm(inv[i].astype(bf16), head_rows(w[i])) for i in range(n)]
    y_all = [ar0[i][C:] + _mm(m_bk[i], jnp.concatenate([head_rows(u[i]), v_rows[i]], axis=0))
             for i in range(n)]
    upd = [_mm(jnp.concatenate([u[i], vp[i]], axis=0).T.astype(bf16),
               (jnp.concatenate([b_t[i], k_t[i]], axis=0) * g_end[i]).astype(bf16)) for i in range(n)]
    for i, (g, p) in enumerate(units):
        sl = lanes_of(p)
        st_ref[g * n_pairs + p] = s0[i] * g_end[i] + jnp.where(same_head, upd[i], 0.0)
        y = y_all[i]
        mean = _seg_sum(y, first_half) * (1.0 / RWKV_HEAD)
        yc = y - mean
        var = _seg_sum(yc * yc, first_half) * (1.0 / RWKV_HEAD)
        yn = yc * lax.rsqrt(var + RWKV_LN_EPS) * lnw_ref[:, sl] + lnb_ref[:, sl]
        bonus = _seg_sum(rkk_all[g][:, sl], first_half) * vp[i]
        y_ref[g, :, sl] = ((yn + bonus) * gate_all[g][:, sl]).astype(y_ref.dtype)


def _rwkv(p, batch, seq, mu, w0, a0, k_k, k_a, lnx_w, lnx_b, r_k, w_wa, g_up_p):
    C = RWKV_CHUNK
    G = RWKV_BATCH_GROUP if batch % RWKV_BATCH_GROUP == 0 else 1
    W = RWKV_WIDTH
    p3 = p.reshape(batch, seq, p.shape[-1])
    row_block = lambda width, col: pl.BlockSpec((G, C, width), lambda b, c: (b, c, col // width))
    vec = lambda width: pl.BlockSpec((1, width), lambda b, c: (0, 0))
    full = lambda arr: pl.BlockSpec(arr.shape, lambda b, c: (0, 0))
    mu_r, mu_k, mu_v, mu_lo = mu
    y = pl.pallas_call(
        _rwkv_kernel,
        grid=(batch // G, seq // C),
        in_specs=[row_block(W, COL_R), row_block(W, COL_K), row_block(W, COL_V),
                  row_block(LORA_BLOCK, COL_LORA),
                  vec(W), vec(W), vec(W), vec(LORA_BLOCK),
                  vec(W), vec(W), vec(W), vec(W), vec(W), vec(W), vec(W),
                  full(w_wa), full(g_up_p)],
        out_specs=pl.BlockSpec((G, C, W), lambda b, c: (b, c, 0)),
        out_shape=jax.ShapeDtypeStruct((batch, seq, W), bf16),
        scratch_shapes=[pltpu.VMEM((G * W // LANES, LANES, LANES), f32),
                        pltpu.VMEM((G, 1, W), f32), pltpu.VMEM((G, 1, W), f32), pltpu.VMEM((G, 1, W), f32),
                        pltpu.VMEM((G, 1, LORA_BLOCK), f32)],
        compiler_params=_params("parallel", "arbitrary"),
        name="rwkv7_chunked",
    )(p3, p3, p3, p3, mu_r, mu_k, mu_v, mu_lo, w0, a0, k_k, k_a, lnx_w, lnx_b, r_k, w_wa, g_up_p)
    return y.reshape(batch * seq, W)


def _rope(x, cos, sin_lo, sin_hi, half):
    n = x.shape[-1]
    return x * cos + pltpu.roll(x, n - half, 1) * sin_lo + pltpu.roll(x, half, 1) * sin_hi


def _dsa_prep_kernel(pos_ref, fa_ref, fi_ref, q_ref, qi_ref, ka_ref, va_ref, kiwi_ref,
                     qo_ref, qio_ref, ko_ref, vo_ref, ki2_ref, wi_ref):
    pos = pos_ref[...]
    lane = lax.broadcasted_iota(jnp.int32, (1, LANES), 1)

    def tables(freq, head, half):
        ang = pos * freq
        cos, sin = jnp.cos(ang), jnp.sin(ang)
        in_head = lane % head
        sin_lo = jnp.where(in_head < half, -sin, 0.0)
        sin_hi = jnp.where((in_head >= half) & (in_head < 2 * half), sin, 0.0)
        return cos, sin_lo, sin_hi

    ca, sla, sha = tables(fa_ref[...], ATT_HEAD, ATT_HEAD // 8)
    ci, sli, shi = tables(fi_ref[...], IDX_HEAD, IDX_HEAD // 8)
    for h in range(ATT_HEADS):
        sl = slice(h * LANES, (h + 1) * LANES)
        qo_ref[:, sl] = _rope(q_ref[:, sl], ca, sla, sha, ATT_HEAD // 8).astype(bf16)
        qio_ref[:, sl] = _rope(qi_ref[:, sl], ci, sli, shi, IDX_HEAD // 8).astype(bf16)
    for h in range(ATT_KV_HEADS):
        sl = slice(h * LANES, (h + 1) * LANES)
        ko_ref[:, sl] = _rope(ka_ref[:, sl], ca, sla, sha, ATT_HEAD // 8).astype(bf16)
    vo_ref[...] = va_ref[...].astype(bf16)
    kiwi = kiwi_ref[...]
    ki_only = jnp.where(lane < IDX_HEAD, kiwi, 0.0)
    ki = _rope(ki_only, ci, sli, shi, IDX_HEAD // 8)
    ki2_ref[...] = (ki + pltpu.roll(ki, IDX_HEAD, 1)).astype(bf16)
    wi = pltpu.roll(kiwi, LANES - IDX_HEAD, 1)
    wi_ref[...] = jnp.where(lane < IDX_HEADS, wi, 0.0) * (IDX_HEADS ** -0.5 * IDX_HEAD ** -0.5)


def _dsa_prep(p, pos, freq_att, freq_idx, tm):
    t = p.shape[0]
    blk = lambda width, col: pl.BlockSpec((tm, width), lambda i: (i, col // width))
    out = lambda width: pl.BlockSpec((tm, width), lambda i: (i, 0))
    vec = pl.BlockSpec((1, LANES), lambda i: (0, 0))
    kvw = ATT_KV_HEADS * ATT_HEAD
    return pl.pallas_call(
        _dsa_prep_kernel,
        grid=(t // tm,),
        in_specs=[pl.BlockSpec((tm, 1), lambda i: (i, 0)), vec, vec,
                  blk(ATT_WIDTH, COL_Q), blk(IDX_HEADS * IDX_HEAD, COL_QI),
                  blk(kvw, COL_KATT), blk(kvw, COL_VATT), blk(LANES, COL_KIWI)],
        out_specs=[out(ATT_WIDTH), out(IDX_HEADS * IDX_HEAD), out(kvw), out(kvw), out(LANES), out(LANES)],
        out_shape=[jax.ShapeDtypeStruct((t, ATT_WIDTH), bf16),
                   jax.ShapeDtypeStruct((t, IDX_HEADS * IDX_HEAD), bf16),
                   jax.ShapeDtypeStruct((t, kvw), bf16),
                   jax.ShapeDtypeStruct((t, kvw), bf16),
                   jax.ShapeDtypeStruct((t, LANES), bf16),
                   jax.ShapeDtypeStruct((t, LANES), f32)],
        compiler_params=_params("parallel"),
        name="dsa_prep",
    )(pos, freq_att, freq_idx, p, p, p, p, p)


def _dsa_kernel(q_ref, qi_ref, wi_ref, k_ref, v_ref, ki2_ref, o_ref, key_ref, *, n_sel, bucket_len):
    tq = q_ref.shape[0]
    seq = k_ref.shape[0]
    bucket = ((pl.program_id(1) + 1) * tq - 1) // bucket_len
    for j in range(seq // bucket_len):
        pl.when(bucket == j)(functools.partial(
            _dsa_body, q_ref, qi_ref, wi_ref, k_ref, v_ref, ki2_ref, o_ref, key_ref,
            n_sel=n_sel, s_len=(j + 1) * bucket_len))


def _dsa_body(q_ref, qi_ref, wi_ref, k_ref, v_ref, ki2_ref, o_ref, key_ref, *, n_sel, s_len):
    tq = q_ref.shape[0]
    q0 = pl.program_id(1) * tq
    lane = lax.broadcasted_iota(jnp.int32, (1, LANES), 1)
    first_half = lane < IDX_HEAD
    ki2 = ki2_ref[0:s_len, :]
    wi = wi_ref[...]

    score = jnp.zeros((tq, s_len), f32)
    for h in range(IDX_HEADS):
        pair = h // 2
        m = first_half if h % 2 == 0 else jnp.logical_not(first_half)
        qm = jnp.where(m, qi_ref[:, pair * LANES:(pair + 1) * LANES], jnp.zeros((), bf16))
        rel = jnp.maximum(_nt(qm, ki2), 0.0)
        w_h = jnp.sum(jnp.where(lane == h, wi, 0.0), axis=-1, keepdims=True)
        score = score + rel * w_h

    q_pos = q0 + lax.broadcasted_iota(jnp.int32, (tq, 1), 0)
    k_pos = lax.broadcasted_iota(jnp.int32, (1, s_len), 1)
    causal = k_pos <= q_pos
    score = jnp.where(causal, score, -1e30) + 0.0
    bits = pltpu.bitcast(score, jnp.int32)
    key_ref[:, 0:s_len] = jnp.where(bits >= 0, bits, bits ^ jnp.int32(0x7FFFFFFF))

    int_min = jnp.int32(-2 ** 31)

    def search(i, t_u):
        cand = t_u | lax.shift_right_logical(int_min, i)
        ge = key_ref[:, 0:s_len] >= (cand ^ int_min)
        cnt = jnp.sum(jnp.where(ge, 1.0, 0.0), axis=-1, keepdims=True)
        return jnp.where(cnt >= n_sel, cand, t_u)

    t_u = lax.fori_loop(0, 32, search, jnp.zeros((tq, 1), jnp.int32), unroll=4)
    selected = (key_ref[:, 0:s_len] >= (t_u ^ int_min)) & causal

    group = ATT_HEADS // ATT_KV_HEADS
    sel_g = jnp.concatenate([selected] * group, axis=0)
    for kv in range(ATT_KV_HEADS):
        k_h = k_ref[0:s_len, kv * LANES:(kv + 1) * LANES]
        v_h = v_ref[0:s_len, kv * LANES:(kv + 1) * LANES]
        qg = jnp.concatenate([q_ref[:, (kv * group + g) * LANES:(kv * group + g + 1) * LANES]
                              for g in range(group)], axis=0)
        s = _nt(qg, k_h) * (ATT_HEAD ** -0.5)
        s = jnp.where(sel_g, s, -jnp.inf)
        s_max = jnp.max(s, axis=-1, keepdims=True)
        e = jnp.exp(s - s_max)
        denom = jnp.sum(e, axis=-1, keepdims=True)
        o = _mm(e.astype(bf16), v_h) / denom
        for g in range(group):
            hq = kv * group + g
            o_ref[:, hq * LANES:(hq + 1) * LANES] = o[g * tq:(g + 1) * tq].astype(o_ref.dtype)


def _dsa(q, qi, wi, k, v, ki2, batch, seq, tq):
    nq = seq // tq
    n_sel = min(TOPK_MAX, seq // 4)
    qblk = lambda width: pl.BlockSpec((tq, width), lambda b, i: (b * nq + i, 0))
    kblk = lambda width: pl.BlockSpec((seq, width), lambda b, i: (b, 0))
    kvw = ATT_KV_HEADS * ATT_HEAD
    bucket_len = max(tq, seq // DSA_BUCKETS)
    return pl.pallas_call(
        functools.partial(_dsa_kernel, n_sel=n_sel, bucket_len=bucket_len),
        grid=(batch, nq),
        in_specs=[qblk(ATT_WIDTH), qblk(IDX_HEADS * IDX_HEAD), qblk(LANES),
                  kblk(kvw), kblk(kvw), kblk(LANES)],
        out_specs=qblk(ATT_WIDTH),
        out_shape=jax.ShapeDtypeStruct((batch * seq, ATT_WIDTH), bf16),
        scratch_shapes=[pltpu.VMEM((tq, seq), jnp.int32)],
        compiler_params=_params("parallel", "arbitrary"),
        name="dsa_attention",
    )(q, qi, wi, k, v, ki2)


def _mix_out_kernel(x_ref, ya_ref, yb_ref, wa_ref, wb_ref, o_ref):
    o_ref[...] = x_ref[...] + _mm(ya_ref[...], wa_ref[...]) + _mm(yb_ref[...], wb_ref[...])


def _mix_out(x, ya, yb, w, tm, tn):
    t, d = x.shape
    ka = ya.shape[1]
    kb = yb.shape[1]
    return pl.pallas_call(
        _mix_out_kernel,
        grid=(t // tm, d // tn),
        in_specs=[pl.BlockSpec((tm, tn), lambda i, j: (i, j)),
                  pl.BlockSpec((tm, ka), lambda i, j: (i, 0)),
                  pl.BlockSpec((tm, kb), lambda i, j: (i, 0)),
                  pl.BlockSpec((ka, tn), lambda i, j: (0, j)),
                  pl.BlockSpec((kb, tn), lambda i, j: (ka // kb, j))],
        out_specs=pl.BlockSpec((tm, tn), lambda i, j: (i, j)),
        out_shape=jax.ShapeDtypeStruct((t, d), f32),
        compiler_params=_params("parallel", "arbitrary"),
        name="mix_out",
    )(x, ya, yb, w, w)


def _cross_kernel(h_ref, g_ref, kv_ref, wq_ref, wo_ref, o_ref):
    h = h_ref[...]
    hn = _rms(h, g_ref[...]).astype(bf16)
    q = _mm(hn, wq_ref[...]).astype(bf16)
    width = CROSS_HEADS * CROSS_HEAD
    outs = []
    for hd in range(CROSS_HEADS):
        sl = slice(hd * CROSS_HEAD, (hd + 1) * CROSS_HEAD)
        k_h = kv_ref[:, sl]
        v_h = kv_ref[:, width + hd * CROSS_HEAD: width + (hd + 1) * CROSS_HEAD]
        s = _nt(q[:, sl], k_h) * (CROSS_HEAD ** -0.5)
        e = jnp.exp(s - jnp.max(s, axis=-1, keepdims=True))
        outs.append((_mm(e.astype(bf16), v_h) / jnp.sum(e, axis=-1, keepdims=True)).astype(bf16))
    o = jnp.concatenate(outs, axis=-1)
    o_ref[...] = h + _mm(o, wo_ref[...])


def _cross(h, g, kv, wq, wo, batch, seq, n_mem, tm):
    d = h.shape[1]
    ns = seq // tm
    width = CROSS_HEADS * CROSS_HEAD
    return pl.pallas_call(
        _cross_kernel,
        grid=(batch, ns),
        in_specs=[pl.BlockSpec((tm, d), lambda b, i: (b * ns + i, 0)),
                  pl.BlockSpec((1, d), lambda b, i: (0, 0)),
                  pl.BlockSpec((n_mem, 2 * width), lambda b, i: (b, 0)),
                  pl.BlockSpec((d, width), lambda b, i: (0, 0)),
                  pl.BlockSpec((width, d), lambda b, i: (0, 0))],
        out_specs=pl.BlockSpec((tm, d), lambda b, i: (b * ns + i, 0)),
        out_shape=jax.ShapeDtypeStruct(h.shape, f32),
        compiler_params=_params("parallel", "arbitrary"),
        name="cross_attention",
    )(h, g, kv, wq, wo)


def _mlp_kernel(h_ref, g_ref, wu_ref, wd_ref, gf_ref, o_ref, hn_ref, *, final_norm):
    j = pl.program_id(1)

    @pl.when(j == 0)
    def _():
        h = h_ref[...]
        hn_ref[...] = _rms(h, g_ref[...]).astype(bf16)
        o_ref[...] = h

    u = jnp.maximum(_mm(hn_ref[...], wu_ref[...]), 0.0)
    o_ref[...] += _mm((u * u).astype(bf16), wd_ref[...])

    if final_norm:
        @pl.when(j == pl.num_programs(1) - 1)
        def _():
            o_ref[...] = _rms(o_ref[...], gf_ref[...])


def _mlp(h, g, wu, wd, gf, final_norm, tm, tf):
    t, d = h.shape
    dff = wu.shape[1]
    return pl.pallas_call(
        functools.partial(_mlp_kernel, final_norm=final_norm),
        grid=(t // tm, dff // tf),
        in_specs=[pl.BlockSpec((tm, d), lambda i, j: (i, 0), pipeline_mode=pl.Buffered(1)),
                  pl.BlockSpec((1, d), lambda i, j: (0, 0)),
                  pl.BlockSpec((d, tf), lambda i, j: (0, j)),
                  pl.BlockSpec((tf, d), lambda i, j: (j, 0)),
                  pl.BlockSpec((1, d), lambda i, j: (0, 0))],
        out_specs=pl.BlockSpec((tm, d), lambda i, j: (i, 0)),
        out_shape=jax.ShapeDtypeStruct((t, d), f32),
        scratch_shapes=[pltpu.VMEM((tm, d), bf16)],
        compiler_params=_params("parallel", "arbitrary"),
        name="mlp_final_norm",
    )(h, g, wu, wd, gf)


def _pack_in_proj(w_in, mu):
    d = w_in.shape[0]
    o = 0
    seg = {}
    for name, width in (("r", 1024), ("k", 1024), ("v", 1024), ("xw", 64), ("xa", 64), ("xg", 160),
                        ("q", 1024), ("katt", 256), ("vatt", 256), ("qi", 1024), ("ki", 64), ("wi", 16)):
        seg[name] = (o, o + width)
        o += width
    w_in = w_in.astype(bf16)
    col = lambda n: w_in[:, seg[n][0]:seg[n][1]]
    zeros = lambda n: jnp.zeros((d, n), w_in.dtype)
    w = jnp.concatenate([col("r"), col("k"), col("v"), col("q"), col("qi"), col("katt"), col("vatt"),
                         col("ki"), col("wi"), zeros(LANES - 80),
                         col("xw"), col("xa"), col("xg"), zeros(LORA_BLOCK - 288)], axis=1)
    mu_lo = jnp.concatenate([mu[3072:3360], jnp.zeros((LORA_BLOCK - 288,), mu.dtype)])
    mus = (mu[0:1024][None], mu[1024:2048][None], mu[2048:3072][None], mu_lo[None])
    return w, mus


def _rope_freq(head, lanes=LANES):
    half = head // 8
    inv_freq = ROPE_THETA ** (-jnp.arange(half, dtype=f32) / half)
    in_head = jnp.arange(lanes) % head
    return jnp.where(in_head < 2 * half, inv_freq[in_head % half], 0.0).astype(f32)[None]


def kernel(x, mem, positions, norm_mix, w_in, rwkv_mu, w_decay0, w_decay_up, a0, a_up, g_up, k_k, k_a, r_k,
           lnx_w, lnx_b, w_mix_out, norm_cross, norm_mem, w_q_cross, w_kv_cross, w_o_cross, norm_mlp,
           w_up, w_down, norm_final):
    batch, seq, d = x.shape
    n_mem = mem.shape[1]
    t = batch * seq
    depth = w_in.shape[0]
    tm = min(512, t)
    h = x.reshape(t, d)
    pos = positions.reshape(t, 1).astype(f32)
    mem2 = mem.reshape(batch * n_mem, d)
    row = lambda vct: vct.reshape(1, -1)

    for l in range(depth):
        w_in_p, mus = _pack_in_proj(w_in[l], rwkv_mu[l])
        zpad = jnp.zeros((DECAY_LORA, RWKV_WIDTH), f32)
        w_wa = jnp.concatenate([jnp.concatenate([w_decay_up[l], zpad], axis=1),
                                jnp.concatenate([zpad, a_up[l]], axis=1)], axis=0).astype(bf16)
        g_up_p = jnp.concatenate([g_up[l], jnp.zeros((LORA_BLOCK - LANES - GATE_LORA, RWKV_WIDTH), f32)],
                                 axis=0).astype(bf16)

        p = _norm_matmul(h, row(norm_mix[l]), w_in_p, f32, min(1024, t), 1024)
        y_rwkv = _rwkv(p, batch, seq, mus, row(w_decay0[l]), row(a0[l]), row(k_k[l]), row(k_a[l]),
                       row(lnx_w[l]), row(lnx_b[l]), row(r_k[l]), w_wa, g_up_p)
        q, qi, k_att, v_att, ki2, wi = _dsa_prep(p, pos, _rope_freq(ATT_HEAD), _rope_freq(IDX_HEAD), tm)
        y_att = _dsa(q, qi, wi, k_att, v_att, ki2, batch, seq, min(128, seq))
        h = _mix_out(h, y_rwkv, y_att, w_mix_out[l].astype(bf16), tm, d)

        kv = _norm_matmul(mem2, row(norm_mem[l]), w_kv_cross[l].astype(bf16), bf16, min(512, batch * n_mem), 1024)
        h = _cross(h, row(norm_cross[l]), kv, w_q_cross[l].astype(bf16), w_o_cross[l].astype(bf16),
                   batch, seq, n_mem, tm)
        h = _mlp(h, row(norm_mlp[l]), w_up[l].astype(bf16), w_down[l].astype(bf16), row(norm_final),
                 l == depth - 1, min(1024, t), 512)
    return h.reshape(batch, seq, d)
```

```python
import functools

import jax
import jax.numpy as jnp
from jax import lax
from jax.experimental import pallas as pl
from jax.experimental.pallas import tpu as pltpu

f32 = jnp.float32
bf16 = jnp.bfloat16

D_MODEL = 2048
RWKV_WIDTH = 1024
RWKV_HEAD = 64
DECAY_LORA = 64
AAA_LORA = 64
GATE_LORA = 160
RWKV_LN_EPS = 64e-5
ATT_WIDTH = 1024
ATT_HEAD = 128
ATT_HEADS = 8
ATT_KV_HEADS = 2
IDX_HEADS = 16
IDX_HEAD = 64
TOPK_MAX = 256
ROPE_THETA = 500000.0
CROSS_HEADS = 4
CROSS_HEAD = 128
D_FF = 4 * D_MODEL
NORM_EPS = 1e-5

LANES = 128
VMEM_LIMIT_BYTES = 56 * 1024 * 1024

COL_R, COL_K, COL_V, COL_Q, COL_QI = 0, 1024, 2048, 3072, 4096
COL_KATT, COL_VATT, COL_KIWI, COL_LORA = 5120, 5376, 5632, 5760
LORA_BLOCK = 384
IN_COLS_PACKED = 6144
RWKV_CHUNK = 64
_EXP_NEG_HALF = 0.6065306597126334
RWKV_BATCH_GROUP = 2
DSA_QUERY_BLOCK = 256
DSA_BUCKETS = 8


def _nt(a, b):
    return lax.dot_general(a, b, (((1,), (1,)), ((), ())), preferred_element_type=f32)


def _mm(a, b):
    return jnp.dot(a, b, preferred_element_type=f32)


def _rms(x, g):
    ms = jnp.mean(x * x, axis=-1, keepdims=True)
    return x * lax.rsqrt(ms + NORM_EPS) * g


def _params(*sem):
    return pltpu.CompilerParams(dimension_semantics=sem, vmem_limit_bytes=VMEM_LIMIT_BYTES)


def _norm_matmul_kernel(x_ref, g_ref, w_ref, o_ref, xn_ref):
    @pl.when(pl.program_id(1) == 0)
    def _():
        xn_ref[...] = _rms(x_ref[...], g_ref[...]).astype(bf16)

    o_ref[...] = _mm(xn_ref[...], w_ref[...]).astype(o_ref.dtype)


def _norm_matmul(x, g, w, out_dtype, tm, tn):
    t, d = x.shape
    n = w.shape[1]
    return pl.pallas_call(
        _norm_matmul_kernel,
        grid=(t // tm, n // tn),
        in_specs=[
            pl.BlockSpec((tm, d), lambda i, j: (i, 0)),
            pl.BlockSpec((1, d), lambda i, j: (0, 0)),
            pl.BlockSpec((d, tn), lambda i, j: (0, j)),
        ],
        out_specs=pl.BlockSpec((tm, tn), lambda i, j: (i, j)),
        out_shape=jax.ShapeDtypeStruct((t, n), out_dtype),
        scratch_shapes=[pltpu.VMEM((tm, d), bf16)],
        compiler_params=_params("parallel", "arbitrary"),
        name="norm_matmul",
    )(x, g, w)


def _seg_sum(x, first_half):
    s_a = jnp.sum(jnp.where(first_half, x, 0.0), axis=-1, keepdims=True)
    s_b = jnp.sum(jnp.where(first_half, 0.0, x), axis=-1, keepdims=True)
    return jnp.where(first_half, s_a, s_b)


def _rwkv_kernel(r_ref, k_ref, v_ref, lo_ref, mur_ref, muk_ref, muv_ref, mulo_ref,
                 w0_ref, a0_ref, kkw_ref, kaw_ref, lnw_ref, lnb_ref, rk_ref, wwa_ref, gup_ref,
                 y_ref, st_ref, pr_ref, pk_ref, pv_ref, plo_ref):
    G, C = r_ref.shape[0], r_ref.shape[1]
    n_pairs = RWKV_WIDTH // LANES

    @pl.when(pl.program_id(1) == 0)
    def _():
        st_ref[...] = jnp.zeros_like(st_ref)
        pr_ref[...] = jnp.zeros_like(pr_ref)
        pk_ref[...] = jnp.zeros_like(pk_ref)
        pv_ref[...] = jnp.zeros_like(pv_ref)
        plo_ref[...] = jnp.zeros_like(plo_ref)

    row = lax.broadcasted_iota(jnp.int32, (C, 1), 0)
    lane = lax.broadcasted_iota(jnp.int32, (1, LANES), 1)
    first_half = lane < RWKV_HEAD
    ti = lax.broadcasted_iota(jnp.int32, (C, C), 0)
    tj = lax.broadcasted_iota(jnp.int32, (C, C), 1)
    tri_b = jnp.where(ti >= tj, 1.0, 0.0).astype(bf16)
    t2 = lax.broadcasted_iota(jnp.int32, (C, 2 * C), 0)
    j2 = lax.broadcasted_iota(jnp.int32, (C, 2 * C), 1) % C
    tri2_strict = t2 > j2
    eye2 = jnp.where(t2 == j2, 1.0, 0.0)
    t4 = lax.broadcasted_iota(jnp.int32, (C, 4 * C), 0)
    j4 = lax.broadcasted_iota(jnp.int32, (C, 4 * C), 1) % C
    tri4_incl = t4 >= j4
    vi = lax.broadcasted_iota(jnp.int32, (LANES, LANES), 0)
    vj = lax.broadcasted_iota(jnp.int32, (LANES, LANES), 1)
    same_head = (vi < RWKV_HEAD) == (vj < RWKV_HEAD)

    def head_rows(x):
        xb = x.astype(bf16)
        zero = jnp.zeros((), bf16)
        return jnp.concatenate([jnp.where(first_half, xb, zero), jnp.where(first_half, zero, xb)], axis=0)

    def shifted(x_ref, prev_ref, mu_ref, g):
        x = x_ref[g]
        prev = jnp.where(row == 0, prev_ref[g], pltpu.roll(x, 1, 0))
        prev_ref[g] = x[C - 1:C, :]
        return x + (prev - x) * mu_ref[...]

    units = [(g, p) for g in range(G) for p in range(n_pairs)]
    lanes_of = lambda p: slice(p * LANES, (p + 1) * LANES)
    v_all, gate_all, rkk_all = [], [], []
    a_t, b_t, k_t, r_t, g_end, s0 = [], [], [], [], [], []
    for g in range(G):
        r = shifted(r_ref, pr_ref, mur_ref, g)
        k = shifted(k_ref, pk_ref, muk_ref, g)
        v = shifted(v_ref, pv_ref, muv_ref, g)
        lo = shifted(lo_ref, plo_ref, mulo_ref, g)
        wa_in = lo[:, :LANES]
        wa_in = jnp.where(first_half, jnp.tanh(wa_in), wa_in).astype(bf16)
        wa = _mm(wa_in, wwa_ref[...])
        logdecay = -_EXP_NEG_HALF / (1.0 + jnp.exp(-(w0_ref[...] + wa[:, :RWKV_WIDTH])))
        a = 1.0 / (1.0 + jnp.exp(-(a0_ref[...] + wa[:, RWKV_WIDTH:])))
        gate_all.append(_mm((1.0 / (1.0 + jnp.exp(-lo[:, LANES:]))).astype(bf16), gup_ref[...]))
        ld_hi = logdecay.astype(bf16)
        ld_r1 = logdecay - ld_hi.astype(f32)
        ld_mid = ld_r1.astype(bf16)
        ld_lo = (ld_r1 - ld_mid.astype(f32)).astype(bf16)
        cum = _mm(tri_b, ld_hi) + _mm(tri_b, ld_mid) + _mm(tri_b, ld_lo)
        kk = k * kkw_ref[...]
        k2 = k * (1.0 + (a - 1.0) * kaw_ref[...])
        rkk_all.append(r * k2 * rk_ref[...])
        v_all.append(v)
        ecum = jnp.exp(cum)
        encum = jnp.exp(-cum)
        ecum_prev = jnp.exp(cum - logdecay)
        for p in range(n_pairs):
            sl = lanes_of(p)
            kkp = kk[:, sl]
            kkn = kkp / jnp.maximum(jnp.sqrt(_seg_sum(kkp * kkp, first_half)), 1e-12)
            a_t.append(-kkn * ecum_prev[:, sl])
            b_t.append(kkn * a[:, sl] * encum[:, sl])
            k_t.append(k2[:, sl] * encum[:, sl])
            r_t.append(r[:, sl] * ecum[:, sl])
            g_end.append(ecum[C - 1:C, sl])
            s0.append(st_ref[g * n_pairs + p])

    n = len(units)
    vp = [v_all[g][:, lanes_of(p)] for g, p in units]
    v_rows = [head_rows(x) for x in vp]
    ar = [jnp.concatenate([a_t[i], r_t[i]], axis=0).astype(bf16) for i in range(n)]
    ar0 = [_nt(ar[i], s0[i].astype(bf16)) for i in range(n)]
    gram = [_nt(ar[i], jnp.concatenate([head_rows(b_t[i]), head_rows(k_t[i])], axis=0)) for i in range(n)]
    l_b = [jnp.where(tri2_strict, x[:C, :2 * C], 0.0) for x in gram]
    l_k = [jnp.where(tri2_strict, x[:C, 2 * C:], 0.0).astype(bf16) for x in gram]
    m_bk = [jnp.where(tri4_incl, x[C:, :], 0.0).astype(bf16) for x in gram]
    w = [ar0[i][:C] + _mm(l_k[i], v_rows[i]) for i in range(n)]
    inv = [eye2 + x for x in l_b]
    l_pow = l_b
    l_pow_rows = [head_rows(x) for x in l_pow]
    for _ in range(C.bit_length() - 2):
        l_pow = [_mm(x.astype(bf16), y) for x, y in zip(l_pow, l_pow_rows)]
        l_pow_rows = [head_rows(x) for x in l_pow]
        inv = [x + _mm(x.astype(bf16), y) for x, y in zip(inv, l_pow_rows)]
    u = [_mm(inv[i].astype(bf16), head_rows(w[i])) for i in range(n)]
    y_all = [ar0[i][C:] + _mm(m_bk[i], jnp.concatenate([head_rows(u[i]), v_rows[i]], axis=0))
             for i in range(n)]
    upd = [_mm(jnp.concatenate([u[i], vp[i]], axis=0).T.astype(bf16),
               (jnp.concatenate([b_t[i], k_t[i]], axis=0) * g_end[i]).astype(bf16)) for i in range(n)]
    for i, (g, p) in enumerate(units):
        sl = lanes_of(p)
        st_ref[g * n_pairs + p] = s0[i] * g_end[i] + jnp.where(same_head, upd[i], 0.0)
        y = y_all[i]
        mean = _seg_sum(y, first_half) * (1.0 / RWKV_HEAD)
        yc = y - mean
        var = _seg_sum(yc * yc, first_half) * (1.0 / RWKV_HEAD)
        yn = yc * lax.rsqrt(var + RWKV_LN_EPS) * lnw_ref[:, sl] + lnb_ref[:, sl]
        bonus = _seg_sum(rkk_all[g][:, sl], first_half) * vp[i]
        y_ref[g, :, sl] = ((yn + bonus) * gate_all[g][:, sl]).astype(y_ref.dtype)


def _rwkv(p, batch, seq, mu, w0, a0, k_k, k_a, lnx_w, lnx_b, r_k, w_wa, g_up_p):
    C = RWKV_CHUNK
    G = RWKV_BATCH_GROUP if batch % RWKV_BATCH_GROUP == 0 else 1
    W = RWKV_WIDTH
    p3 = p.reshape(batch, seq, p.shape[-1])
    row_block = lambda width, col: pl.BlockSpec((G, C, width), lambda b, c: (b, c, col // width))
    vec = lambda width: pl.BlockSpec((1, width), lambda b, c: (0, 0))
    full = lambda arr: pl.BlockSpec(arr.shape, lambda b, c: (0, 0))
    mu_r, mu_k, mu_v, mu_lo = mu
    y = pl.pallas_call(
        _rwkv_kernel,
        grid=(batch // G, seq // C),
        in_specs=[row_block(W, COL_R), row_block(W, COL_K), row_block(W, COL_V),
                  row_block(LORA_BLOCK, COL_LORA),
                  vec(W), vec(W), vec(W), vec(LORA_BLOCK),
                  vec(W), vec(W), vec(W), vec(W), vec(W), vec(W), vec(W),
                  full(w_wa), full(g_up_p)],
        out_specs=pl.BlockSpec((G, C, W), lambda b, c: (b, c, 0)),
        out_shape=jax.ShapeDtypeStruct((batch, seq, W), bf16),
        scratch_shapes=[pltpu.VMEM((G * W // LANES, LANES, LANES), f32),
                        pltpu.VMEM((G, 1, W), f32), pltpu.VMEM((G, 1, W), f32), pltpu.VMEM((G, 1, W), f32),
                        pltpu.VMEM((G, 1, LORA_BLOCK), f32)],
        compiler_params=_params("parallel", "arbitrary"),
        name="rwkv7_chunked",
    )(p3, p3, p3, p3, mu_r, mu_k, mu_v, mu_lo, w0, a0, k_k, k_a, lnx_w, lnx_b, r_k, w_wa, g_up_p)
    return y.reshape(batch * seq, W)


def _rope(x, cos, sin_lo, sin_hi, half):
    n = x.shape[-1]
    return x * cos + pltpu.roll(x, n - half, 1) * sin_lo + pltpu.roll(x, half, 1) * sin_hi


def _dsa_prep_kernel(pos_ref, fa_ref, fi_ref, q_ref, qi_ref, ka_ref, va_ref, kiwi_ref,
                     qo_ref, qio_ref, ko_ref, vo_ref, ki2_ref, wi_ref):
    pos = pos_ref[...]
    lane = lax.broadcasted_iota(jnp.int32, (1, LANES), 1)

    def tables(freq, head, half):
        ang = pos * freq
        cos, sin = jnp.cos(ang), jnp.sin(ang)
        in_head = lane % head
        sin_lo = jnp.where(in_head < half, -sin, 0.0)
        sin_hi = jnp.where((in_head >= half) & (in_head < 2 * half), sin, 0.0)
        return cos, sin_lo, sin_hi

    ca, sla, sha = tables(fa_ref[...], ATT_HEAD, ATT_HEAD // 8)
    ci, sli, shi = tables(fi_ref[...], IDX_HEAD, IDX_HEAD // 8)
    for h in range(ATT_HEADS):
        sl = slice(h * LANES, (h + 1) * LANES)
        qo_ref[:, sl] = _rope(q_ref[:, sl], ca, sla, sha, ATT_HEAD // 8).astype(bf16)
        qio_ref[:, sl] = _rope(qi_ref[:, sl], ci, sli, shi, IDX_HEAD // 8).astype(bf16)
    for h in range(ATT_KV_HEADS):
        sl = slice(h * LANES, (h + 1) * LANES)
        ko_ref[:, sl] = _rope(ka_ref[:, sl], ca, sla, sha, ATT_HEAD // 8).astype(bf16)
    vo_ref[...] = va_ref[...].astype(bf16)
    kiwi = kiwi_ref[...]
    ki_only = jnp.where(lane < IDX_HEAD, kiwi, 0.0)
    ki = _rope(ki_only, ci, sli, shi, IDX_HEAD // 8)
    ki2_ref[...] = (ki + pltpu.roll(ki, IDX_HEAD, 1)).astype(bf16)
    wi = pltpu.roll(kiwi, LANES - IDX_HEAD, 1)
    wi_ref[...] = jnp.where(lane < IDX_HEADS, wi, 0.0) * (IDX_HEADS ** -0.5 * IDX_HEAD ** -0.5)


def _dsa_prep(p, pos, freq_att, freq_idx, tm):
    t = p.shape[0]
    blk = lambda width, col: pl.BlockSpec((tm, width), lambda i: (i, col // width))
    out = lambda width: pl.BlockSpec((tm, width), lambda i: (i, 0))
    vec = pl.BlockSpec((1, LANES), lambda i: (0, 0))
    kvw = ATT_KV_HEADS * ATT_HEAD
    return pl.pallas_call(
        _dsa_prep_kernel,
        grid=(t // tm,),
        in_specs=[pl.BlockSpec((tm, 1), lambda i: (i, 0)), vec, vec,
                  blk(ATT_WIDTH, COL_Q), blk(IDX_HEADS * IDX_HEAD, COL_QI),
                  blk(kvw, COL_KATT), blk(kvw, COL_VATT), blk(LANES, COL_KIWI)],
        out_specs=[out(ATT_WIDTH), out(IDX_HEADS * IDX_HEAD), out(kvw), out(kvw), out(LANES), out(LANES)],
        out_shape=[jax.ShapeDtypeStruct((t, ATT_WIDTH), bf16),
                   jax.ShapeDtypeStruct((t, IDX_HEADS * IDX_HEAD), bf16),
                   jax.ShapeDtypeStruct((t, kvw), bf16),
                   jax.ShapeDtypeStruct((t, kvw), bf16),
                   jax.ShapeDtypeStruct((t, LANES), bf16),
                   jax.ShapeDtypeStruct((t, LANES), f32)],
        compiler_params=_params("parallel"),
        name="dsa_prep",
    )(pos, freq_att, freq_idx, p, p, p, p, p)


def _dsa_kernel(q_ref, qi_ref, wi_ref, k_ref, v_ref, ki2_ref, o_ref, key_ref, *, n_sel, bucket_len):
    tq = q_ref.shape[0]
    seq = k_ref.shape[0]
    bucket = ((pl.program_id(1) + 1) * tq - 1) // bucket_len
    for j in range(seq // bucket_len):
        pl.when(bucket == j)(functools.partial(
            _dsa_body, q_ref, qi_ref, wi_ref, k_ref, v_ref, ki2_ref, o_ref, key_ref,
            n_sel=n_sel, s_len=(j + 1) * bucket_len))


def _dsa_body(q_ref, qi_ref, wi_ref, k_ref, v_ref, ki2_ref, o_ref, key_ref, *, n_sel, s_len):
    tq = q_ref.shape[0]
    q0 = pl.program_id(1) * tq
    lane = lax.broadcasted_iota(jnp.int32, (1, LANES), 1)
    first_half = lane < IDX_HEAD
    ki2 = ki2_ref[0:s_len, :]
    wi = wi_ref[...]

    score = jnp.zeros((tq, s_len), f32)
    for h in range(IDX_HEADS):
        pair = h // 2
        m = first_half if h % 2 == 0 else jnp.logical_not(first_half)
        qm = jnp.where(m, qi_ref[:, pair * LANES:(pair + 1) * LANES], jnp.zeros((), bf16))
        rel = jnp.maximum(_nt(qm, ki2), 0.0)
        w_h = jnp.sum(jnp.where(lane == h, wi, 0.0), axis=-1, keepdims=True)
        score = score + rel * w_h

    q_pos = q0 + lax.broadcasted_iota(jnp.int32, (tq, 1), 0)
    k_pos = lax.broadcasted_iota(jnp.int32, (1, s_len), 1)
    causal = k_pos <= q_pos
    score = jnp.where(causal, score, -1e30) + 0.0
    bits = pltpu.bitcast(score, jnp.int32)
    key_ref[:, 0:s_len] = jnp.where(bits >= 0, bits, bits ^ jnp.int32(0x7FFFFFFF))

    int_min = jnp.int32(-2 ** 31)

    def search(i, t_u):
        cand = t_u | lax.shift_right_logical(int_min, i)
        ge = key_ref[:, 0:s_len] >= (cand ^ int_min)
        cnt = jnp.sum(jnp.where(ge, 1.0, 0.0), axis=-1, keepdims=True)
        return jnp.where(cnt >= n_sel, cand, t_u)

    t_u = lax.fori_loop(0, 32, search, jnp.zeros((tq, 1), jnp.int32), unroll=4)
    selected = (key_ref[:, 0:s_len] >= (t_u ^ int_min)) & causal

    group = ATT_HEADS // ATT_KV_HEADS
    sel_g = jnp.concatenate([selected] * group, axis=0)
    for kv in range(ATT_KV_HEADS):
        k_h = k_ref[0:s_len, kv * LANES:(kv + 1) * LANES]
        v_h = v_ref[0:s_len, kv * LANES:(kv + 1) * LANES]
        qg = jnp.concatenate([q_ref[:, (kv * group + g) * LANES:(kv * group + g + 1) * LANES]
                              for g in range(group)], axis=0)
        s = _nt(qg, k_h) * (ATT_HEAD ** -0.5)
        s = jnp.where(sel_g, s, -jnp.inf)
        s_max = jnp.max(s, axis=-1, keepdims=True)
        e = jnp.exp(s - s_max)
        denom = jnp.sum(e, axis=-1, keepdims=True)
        o = _mm(e.astype(bf16), v_h) / denom
        for g in range(group):
            hq = kv * group + g
            o_ref[:, hq * LANES:(hq + 1) * LANES] = o[g * tq:(g + 1) * tq].astype(o_ref.dtype)


def _dsa(q, qi, wi, k, v, ki2, batch, seq, tq):
    nq = seq // tq
    n_sel = min(TOPK_MAX, seq // 4)
    qblk = lambda width: pl.BlockSpec((tq, width), lambda b, i: (b * nq + i, 0))
    kblk = lambda width: pl.BlockSpec((seq, width), lambda b, i: (b, 0))
    kvw = ATT_KV_HEADS * ATT_HEAD
    bucket_len = max(tq, seq // DSA_BUCKETS)
    return pl.pallas_call(
        functools.partial(_dsa_kernel, n_sel=n_sel, bucket_len=bucket_len),
        grid=(batch, nq),
        in_specs=[qblk(ATT_WIDTH), qblk(IDX_HEADS * IDX_HEAD), qblk(LANES),
                  kblk(kvw), kblk(kvw), kblk(LANES)],
        out_specs=qblk(ATT_WIDTH),
        out_shape=jax.ShapeDtypeStruct((batch * seq, ATT_WIDTH), bf16),
        scratch_shapes=[pltpu.VMEM((tq, seq), jnp.int32)],
        compiler_params=_params("parallel", "arbitrary"),
        name="dsa_attention",
    )(q, qi, wi, k, v, ki2)


def _mix_out_kernel(x_ref, ya_ref, yb_ref, wa_ref, wb_ref, o_ref):
    o_ref[...] = x_ref[...] + _mm(ya_ref[...], wa_ref[...]) + _mm(yb_ref[...], wb_ref[...])


def _mix_out(x, ya, yb, w, tm, tn):
    t, d = x.shape
    ka = ya.shape[1]
    kb = yb.shape[1]
    return pl.pallas_call(
        _mix_out_kernel,
        grid=(t // tm, d // tn),
        in_specs=[pl.BlockSpec((tm, tn), lambda i, j: (i, j)),
                  pl.BlockSpec((tm, ka), lambda i, j: (i, 0)),
                  pl.BlockSpec((tm, kb), lambda i, j: (i, 0)),
                  pl.BlockSpec((ka, tn), lambda i, j: (0, j)),
                  pl.BlockSpec((kb, tn), lambda i, j: (ka // kb, j))],
        out_specs=pl.BlockSpec((tm, tn), lambda i, j: (i, j)),
        out_shape=jax.ShapeDtypeStruct((t, d), f32),
        compiler_params=_params("parallel", "arbitrary"),
        name="mix_out",
    )(x, ya, yb, w, w)


def _cross_kernel(h_ref, g_ref, kv_ref, wq_ref, wo_ref, o_ref):
    h = h_ref[...]
    hn = _rms(h, g_ref[...]).astype(bf16)
    q = _mm(hn, wq_ref[...]).astype(bf16)
    width = CROSS_HEADS * CROSS_HEAD
    outs = []
    for hd in range(CROSS_HEADS):
        sl = slice(hd * CROSS_HEAD, (hd + 1) * CROSS_HEAD)
        k_h = kv_ref[:, sl]
        v_h = kv_ref[:, width + hd * CROSS_HEAD: width + (hd + 1) * CROSS_HEAD]
        s = _nt(q[:, sl], k_h) * (CROSS_HEAD ** -0.5)
        e = jnp.exp(s - jnp.max(s, axis=-1, keepdims=True))
        outs.append((_mm(e.astype(bf16), v_h) / jnp.sum(e, axis=-1, keepdims=True)).astype(bf16))
    o = jnp.concatenate(outs, axis=-1)
    o_ref[...] = h + _mm(o, wo_ref[...])


def _cross(h, g, kv, wq, wo, batch, seq, n_mem, tm):
    d = h.shape[1]
    ns = seq // tm
    width = CROSS_HEADS * CROSS_HEAD
    return pl.pallas_call(
        _cross_kernel,
        grid=(batch, ns),
        in_specs=[pl.BlockSpec((tm, d), lambda b, i: (b * ns + i, 0)),
                  pl.BlockSpec((1, d), lambda b, i: (0, 0)),
                  pl.BlockSpec((n_mem, 2 * width), lambda b, i: (b, 0)),
                  pl.BlockSpec((d, width), lambda b, i: (0, 0)),
                  pl.BlockSpec((width, d), lambda b, i: (0, 0))],
        out_specs=pl.BlockSpec((tm, d), lambda b, i: (b * ns + i, 0)),
        out_shape=jax.ShapeDtypeStruct(h.shape, f32),
        compiler_params=_params("parallel", "arbitrary"),
        name="cross_attention",
    )(h, g, kv, wq, wo)


def _mlp_kernel(h_ref, g_ref, wu_ref, wd_ref, gf_ref, o_ref, hn_ref, *, final_norm):
    j = pl.program_id(1)

    @pl.when(j == 0)
    def _():
        h = h_ref[...]
        hn_ref[...] = _rms(h, g_ref[...]).astype(bf16)
        o_ref[...] = h

    u = jnp.maximum(_mm(hn_ref[...], wu_ref[...]), 0.0)
    o_ref[...] += _mm((u * u).astype(bf16), wd_ref[...])

    if final_norm:
        @pl.when(j == pl.num_programs(1) - 1)
        def _():
            o_ref[...] = _rms(o_ref[...], gf_ref[...])


def _mlp(h, g, wu, wd, gf, final_norm, tm, tf):
    t, d = h.shape
    dff = wu.shape[1]
    return pl.pallas_call(
        functools.partial(_mlp_kernel, final_norm=final_norm),
        grid=(t // tm, dff // tf),
        in_specs=[pl.BlockSpec((tm, d), lambda i, j: (i, 0)),
                  pl.BlockSpec((1, d), lambda i, j: (0, 0)),
                  pl.BlockSpec((d, tf), lambda i, j: (0, j)),
                  pl.BlockSpec((tf, d), lambda i, j: (j, 0)),
                  pl.BlockSpec((1, d), lambda i, j: (0, 0))],
        out_specs=pl.BlockSpec((tm, d), lambda i, j: (i, 0)),
        out_shape=jax.ShapeDtypeStruct((t, d), f32),
        scratch_shapes=[pltpu.VMEM((tm, d), bf16)],
        compiler_params=_params("parallel", "arbitrary"),
        name="mlp_final_norm",
    )(h, g, wu, wd, gf)


def _pack_in_proj(w_in, mu):
    d = w_in.shape[0]
    o = 0
    seg = {}
    for name, width in (("r", 1024), ("k", 1024), ("v", 1024), ("xw", 64), ("xa", 64), ("xg", 160),
                        ("q", 1024), ("katt", 256), ("vatt", 256), ("qi", 1024), ("ki", 64), ("wi", 16)):
        seg[name] = (o, o + width)
        o += width
    w_in = w_in.astype(bf16)
    col = lambda n: w_in[:, seg[n][0]:seg[n][1]]
    zeros = lambda n: jnp.zeros((d, n), w_in.dtype)
    w = jnp.concatenate([col("r"), col("k"), col("v"), col("q"), col("qi"), col("katt"), col("vatt"),
                         col("ki"), col("wi"), zeros(LANES - 80),
                         col("xw"), col("xa"), col("xg"), zeros(LORA_BLOCK - 288)], axis=1)
    mu_lo = jnp.concatenate([mu[3072:3360], jnp.zeros((LORA_BLOCK - 288,), mu.dtype)])
    mus = (mu[0:1024][None], mu[1024:2048][None], mu[2048:3072][None], mu_lo[None])
    return w, mus


def _rope_freq(head, lanes=LANES):
    half = head // 8
    inv_freq = ROPE_THETA ** (-jnp.arange(half, dtype=f32) / half)
    in_head = jnp.arange(lanes) % head
    return jnp.where(in_head < 2 * half, inv_freq[in_head % half], 0.0).astype(f32)[None]


def kernel(x, mem, positions, norm_mix, w_in, rwkv_mu, w_decay0, w_decay_up, a0, a_up, g_up, k_k, k_a, r_k,
           lnx_w, lnx_b, w_mix_out, norm_cross, norm_mem, w_q_cross, w_kv_cross, w_o_cross, norm_mlp,
           w_up, w_down, norm_final):
    batch, seq, d = x.shape
    n_mem = mem.shape[1]
    t = batch * seq
    depth = w_in.shape[0]
    tm = min(512, t)
    h = x.reshape(t, d)
    pos = positions.reshape(t, 1).astype(f32)
    mem2 = mem.reshape(batch * n_mem, d)
    row = lambda vct: vct.reshape(1, -1)

    for l in range(depth):
        w_in_p, mus = _pack_in_proj(w_in[l], rwkv_mu[l])
        zpad = jnp.zeros((DECAY_LORA, RWKV_WIDTH), f32)
        w_wa = jnp.concatenate([jnp.concatenate([w_decay_up[l], zpad], axis=1),
                                jnp.concatenate([zpad, a_up[l]], axis=1)], axis=0).astype(bf16)
        g_up_p = jnp.concatenate([g_up[l], jnp.zeros((LORA_BLOCK - LANES - GATE_LORA, RWKV_WIDTH), f32)],
                                 axis=0).astype(bf16)

        p = _norm_matmul(h, row(norm_mix[l]), w_in_p, f32, min(1024, t), 1024)
        y_rwkv = _rwkv(p, batch, seq, mus, row(w_decay0[l]), row(a0[l]), row(k_k[l]), row(k_a[l]),
                       row(lnx_w[l]), row(lnx_b[l]), row(r_k[l]), w_wa, g_up_p)
        q, qi, k_att, v_att, ki2, wi = _dsa_prep(p, pos, _rope_freq(ATT_HEAD), _rope_freq(IDX_HEAD), tm)
        y_att = _dsa(q, qi, wi, k_att, v_att, ki2, batch, seq, min(DSA_QUERY_BLOCK, seq))
        h = _mix_out(h, y_rwkv, y_att, w_mix_out[l].astype(bf16), tm, d)

        kv = _norm_matmul(mem2, row(norm_mem[l]), w_kv_cross[l].astype(bf16), bf16, min(512, batch * n_mem), 1024)
        h = _cross(h, row(norm_cross[l]), kv, w_q_cross[l].astype(bf16), w_o_cross[l].astype(bf16),
                   batch, seq, n_mem, tm)
        h = _mlp(h, row(norm_mlp[l]), w_up[l].astype(bf16), w_down[l].astype(bf16), row(norm_final),
                 l == depth - 1, tm, 1024)
    return h.reshape(batch, seq, d)
```

```python
import functools

import jax
import jax.numpy as jnp
from jax import lax
from jax.experimental import pallas as pl
from jax.experimental.pallas import tpu as pltpu

f32 = jnp.float32
bf16 = jnp.bfloat16

D_MODEL = 2048
RWKV_WIDTH = 1024
RWKV_HEAD = 64
DECAY_LORA = 64
AAA_LORA = 64
GATE_LORA = 160
RWKV_LN_EPS = 64e-5
ATT_WIDTH = 1024
ATT_HEAD = 128
ATT_HEADS = 8
ATT_KV_HEADS = 2
IDX_HEADS = 16
IDX_HEAD = 64
TOPK_MAX = 256
ROPE_THETA = 500000.0
CROSS_HEADS = 4
CROSS_HEAD = 128
D_FF = 4 * D_MODEL
NORM_EPS = 1e-5

LANES = 128
SUBLANES = 8
VMEM_LIMIT_BYTES = 56 * 1024 * 1024

COL_R, COL_K, COL_V, COL_Q, COL_QI = 0, 1024, 2048, 3072, 4096
COL_KATT, COL_VATT, COL_KIWI, COL_LORA = 5120, 5376, 5632, 5760
LORA_BLOCK = 384
IN_COLS_PACKED = 6144
RWKV_CHUNK = 64
_EXP_NEG_HALF = 0.6065306597126334
RWKV_BATCH_GROUP = 2
DSA_QUERY_BLOCK = 128
COUNT_CHAINS = 8
DSA_BUCKETS = 8


def _nt(a, b):
    return lax.dot_general(a, b, (((1,), (1,)), ((), ())), preferred_element_type=f32)


def _mm(a, b):
    return jnp.dot(a, b, preferred_element_type=f32)


def _rms(x, g):
    ms = jnp.mean(x * x, axis=-1, keepdims=True)
    return x * lax.rsqrt(ms + NORM_EPS) * g


def _params(*sem):
    return pltpu.CompilerParams(dimension_semantics=sem, vmem_limit_bytes=VMEM_LIMIT_BYTES)


def _norm_matmul_kernel(x_ref, g_ref, w_ref, o_ref, xn_ref):
    @pl.when(pl.program_id(1) == 0)
    def _():
        xn_ref[...] = _rms(x_ref[...], g_ref[...]).astype(bf16)

    o_ref[...] = _mm(xn_ref[...], w_ref[...]).astype(o_ref.dtype)


def _norm_matmul(x, g, w, out_dtype, tm, tn):
    t, d = x.shape
    n = w.shape[1]
    return pl.pallas_call(
        _norm_matmul_kernel,
        grid=(t // tm, n // tn),
        in_specs=[
            pl.BlockSpec((tm, d), lambda i, j: (i, 0)),
            pl.BlockSpec((1, d), lambda i, j: (0, 0)),
            pl.BlockSpec((d, tn), lambda i, j: (0, j)),
        ],
        out_specs=pl.BlockSpec((tm, tn), lambda i, j: (i, j)),
        out_shape=jax.ShapeDtypeStruct((t, n), out_dtype),
        scratch_shapes=[pltpu.VMEM((tm, d), bf16)],
        compiler_params=_params("parallel", "arbitrary"),
        name="norm_matmul",
    )(x, g, w)


def _seg_sum(x, first_half):
    s_a = jnp.sum(jnp.where(first_half, x, 0.0), axis=-1, keepdims=True)
    s_b = jnp.sum(jnp.where(first_half, 0.0, x), axis=-1, keepdims=True)
    return jnp.where(first_half, s_a, s_b)


def _rwkv_kernel(r_ref, k_ref, v_ref, lo_ref, mur_ref, muk_ref, muv_ref, mulo_ref,
                 w0_ref, a0_ref, kkw_ref, kaw_ref, lnw_ref, lnb_ref, rk_ref, wwa_ref, gup_ref,
                 y_ref, st_ref, pr_ref, pk_ref, pv_ref, plo_ref):
    G, C = r_ref.shape[0], r_ref.shape[1]
    n_pairs = RWKV_WIDTH // LANES

    @pl.when(pl.program_id(1) == 0)
    def _():
        st_ref[...] = jnp.zeros_like(st_ref)
        pr_ref[...] = jnp.zeros_like(pr_ref)
        pk_ref[...] = jnp.zeros_like(pk_ref)
        pv_ref[...] = jnp.zeros_like(pv_ref)
        plo_ref[...] = jnp.zeros_like(plo_ref)

    row = lax.broadcasted_iota(jnp.int32, (C, 1), 0)
    lane = lax.broadcasted_iota(jnp.int32, (1, LANES), 1)
    first_half = lane < RWKV_HEAD
    ti = lax.broadcasted_iota(jnp.int32, (C, C), 0)
    tj = lax.broadcasted_iota(jnp.int32, (C, C), 1)
    tri_b = jnp.where(ti >= tj, 1.0, 0.0).astype(bf16)
    t2 = lax.broadcasted_iota(jnp.int32, (C, 2 * C), 0)
    j2 = lax.broadcasted_iota(jnp.int32, (C, 2 * C), 1) % C
    tri2_strict = t2 > j2
    eye2 = jnp.where(t2 == j2, 1.0, 0.0)
    t4 = lax.broadcasted_iota(jnp.int32, (C, 4 * C), 0)
    j4 = lax.broadcasted_iota(jnp.int32, (C, 4 * C), 1) % C
    tri4_incl = t4 >= j4
    vi = lax.broadcasted_iota(jnp.int32, (LANES, LANES), 0)
    vj = lax.broadcasted_iota(jnp.int32, (LANES, LANES), 1)
    same_head = (vi < RWKV_HEAD) == (vj < RWKV_HEAD)

    def head_rows(x):
        xb = x.astype(bf16)
        zero = jnp.zeros((), bf16)
        return jnp.concatenate([jnp.where(first_half, xb, zero), jnp.where(first_half, zero, xb)], axis=0)

    def shifted(x_ref, prev_ref, mu_ref, g):
        x = x_ref[g]
        prev = jnp.where(row == 0, prev_ref[g], pltpu.roll(x, 1, 0))
        prev_ref[g] = x[C - 1:C, :]
        return x + (prev - x) * mu_ref[...]

    units = [(g, p) for g in range(G) for p in range(n_pairs)]
    lanes_of = lambda p: slice(p * LANES, (p + 1) * LANES)
    v_all, gate_all, rkk_all = [], [], []
    a_t, b_t, k_t, r_t, g_end, s0 = [], [], [], [], [], []
    for g in range(G):
        r = shifted(r_ref, pr_ref, mur_ref, g)
        k = shifted(k_ref, pk_ref, muk_ref, g)
        v = shifted(v_ref, pv_ref, muv_ref, g)
        lo = shifted(lo_ref, plo_ref, mulo_ref, g)
        wa_in = lo[:, :LANES]
        wa_in = jnp.where(first_half, jnp.tanh(wa_in), wa_in).astype(bf16)
        wa = _mm(wa_in, wwa_ref[...])
        logdecay = -_EXP_NEG_HALF / (1.0 + jnp.exp(-(w0_ref[...] + wa[:, :RWKV_WIDTH])))
        a = 1.0 / (1.0 + jnp.exp(-(a0_ref[...] + wa[:, RWKV_WIDTH:])))
        gate_all.append(_mm((1.0 / (1.0 + jnp.exp(-lo[:, LANES:]))).astype(bf16), gup_ref[...]))
        ld_hi = logdecay.astype(bf16)
        ld_r1 = logdecay - ld_hi.astype(f32)
        ld_mid = ld_r1.astype(bf16)
        ld_lo = (ld_r1 - ld_mid.astype(f32)).astype(bf16)
        cum = _mm(tri_b, ld_hi) + _mm(tri_b, ld_mid) + _mm(tri_b, ld_lo)
        kk = k * kkw_ref[...]
        k2 = k * (1.0 + (a - 1.0) * kaw_ref[...])
        rkk_all.append(r * k2 * rk_ref[...])
        v_all.append(v)
        ecum = jnp.exp(cum)
        encum = jnp.exp(-cum)
        ecum_prev = jnp.exp(cum - logdecay)
        for p in range(n_pairs):
            sl = lanes_of(p)
            kkp = kk[:, sl]
            kkn = kkp * lax.rsqrt(jnp.maximum(_seg_sum(kkp * kkp, first_half), 1e-24))
            a_t.append(-kkn * ecum_prev[:, sl])
            b_t.append(kkn * a[:, sl] * encum[:, sl])
            k_t.append(k2[:, sl] * encum[:, sl])
            r_t.append(r[:, sl] * ecum[:, sl])
            g_end.append(ecum[C - 1:C, sl])
            s0.append(st_ref[g * n_pairs + p])

    n = len(units)
    vp = [v_all[g][:, lanes_of(p)] for g, p in units]
    v_rows = [head_rows(x) for x in vp]
    ar = [jnp.concatenate([a_t[i], r_t[i]], axis=0).astype(bf16) for i in range(n)]
    ar0 = [_nt(ar[i], s0[i].astype(bf16)) for i in range(n)]
    gram = [_nt(ar[i], jnp.concatenate([head_rows(b_t[i]), head_rows(k_t[i])], axis=0)) for i in range(n)]
    l_b = [jnp.where(tri2_strict, x[:C, :2 * C], 0.0) for x in gram]
    l_k = [jnp.where(tri2_strict, x[:C, 2 * C:], 0.0).astype(bf16) for x in gram]
    m_bk = [jnp.where(tri4_incl, x[C:, :], 0.0).astype(bf16) for x in gram]
    w = [ar0[i][:C] + _mm(l_k[i], v_rows[i]) for i in range(n)]
    inv = [eye2 + x for x in l_b]
    l_pow = l_b
    l_pow_rows = [head_rows(x) for x in l_pow]
    for _ in range(C.bit_length() - 2):
        l_pow = [_mm(x.astype(bf16), y) for x, y in zip(l_pow, l_pow_rows)]
        l_pow_rows = [head_rows(x) for x in l_pow]
        inv = [x + _mm(x.astype(bf16), y) for x, y in zip(inv, l_pow_rows)]
    u = [_mm(inv[i].astype(bf16), head_rows(w[i])) for i in range(n)]
    y_all = [ar0[i][C:] + _mm(m_bk[i], jnp.concatenate([head_rows(u[i]), v_rows[i]], axis=0))
             for i in range(n)]
    upd = [_mm(jnp.concatenate([u[i], vp[i]], axis=0).T.astype(bf16),
               (jnp.concatenate([b_t[i], k_t[i]], axis=0) * g_end[i]).astype(bf16)) for i in range(n)]
    for i, (g, p) in enumerate(units):
        sl = lanes_of(p)
        st_ref[g * n_pairs + p] = s0[i] * g_end[i] + jnp.where(same_head, upd[i], 0.0)
        y = y_all[i]
        mean = _seg_sum(y, first_half) * (1.0 / RWKV_HEAD)
        yc = y - mean
        var = _seg_sum(yc * yc, first_half) * (1.0 / RWKV_HEAD)
        yn = yc * lax.rsqrt(var + RWKV_LN_EPS) * lnw_ref[:, sl] + lnb_ref[:, sl]
        bonus = _seg_sum(rkk_all[g][:, sl], first_half) * vp[i]
        y_ref[g, :, sl] = ((yn + bonus) * gate_all[g][:, sl]).astype(y_ref.dtype)


def _rwkv(p, batch, seq, mu, w0, a0, k_k, k_a, lnx_w, lnx_b, r_k, w_wa, g_up_p):
    C = RWKV_CHUNK
    G = RWKV_BATCH_GROUP if batch % RWKV_BATCH_GROUP == 0 else 1
    W = RWKV_WIDTH
    p3 = p.reshape(batch, seq, p.shape[-1])
    row_block = lambda width, col: pl.BlockSpec((G, C, width), lambda b, c: (b, c, col // width))
    vec = lambda width: pl.BlockSpec((1, width), lambda b, c: (0, 0))
    full = lambda arr: pl.BlockSpec(arr.shape, lambda b, c: (0, 0))
    mu_r, mu_k, mu_v, mu_lo = mu
    y = pl.pallas_call(
        _rwkv_kernel,
        grid=(batch // G, seq // C),
        in_specs=[row_block(W, COL_R), row_block(W, COL_K), row_block(W, COL_V),
                  row_block(LORA_BLOCK, COL_LORA),
                  vec(W), vec(W), vec(W), vec(LORA_BLOCK),
                  vec(W), vec(W), vec(W), vec(W), vec(W), vec(W), vec(W),
                  full(w_wa), full(g_up_p)],
        out_specs=pl.BlockSpec((G, C, W), lambda b, c: (b, c, 0)),
        out_shape=jax.ShapeDtypeStruct((batch, seq, W), bf16),
        scratch_shapes=[pltpu.VMEM((G * W // LANES, LANES, LANES), f32),
                        pltpu.VMEM((G, 1, W), f32), pltpu.VMEM((G, 1, W), f32), pltpu.VMEM((G, 1, W), f32),
                        pltpu.VMEM((G, 1, LORA_BLOCK), f32)],
        compiler_params=_params("parallel", "arbitrary"),
        name="rwkv7_chunked",
    )(p3, p3, p3, p3, mu_r, mu_k, mu_v, mu_lo, w0, a0, k_k, k_a, lnx_w, lnx_b, r_k, w_wa, g_up_p)
    return y.reshape(batch * seq, W)


def _rope(x, cos, sin_lo, sin_hi, half):
    n = x.shape[-1]
    return x * cos + pltpu.roll(x, n - half, 1) * sin_lo + pltpu.roll(x, half, 1) * sin_hi


def _dsa_prep_kernel(pos_ref, freq_ref, q_ref, qi_ref, ka_ref, va_ref, kiwi_ref,
                     qo_ref, qio_ref, ko_ref, vo_ref, ki2_ref, wi_ref):
    pos = pos_ref[...]
    lane = lax.broadcasted_iota(jnp.int32, (1, LANES), 1)
    ang = pos * freq_ref[...]
    cos_t, sin_t = jnp.cos(ang), jnp.sin(ang)

    def tables(cos, sin, head, half):
        in_head = lane % head
        sin_lo = jnp.where(in_head < half, -sin, 0.0)
        sin_hi = jnp.where((in_head >= half) & (in_head < 2 * half), sin, 0.0)
        return cos, sin_lo, sin_hi

    att_lanes = lane < ATT_HEAD // 4
    upper = lane >= IDX_HEAD
    ca, sla, sha = tables(jnp.where(att_lanes, cos_t, 1.0), jnp.where(att_lanes, sin_t, 0.0),
                          ATT_HEAD, ATT_HEAD // 8)
    ci, sli, shi = tables(jnp.where(upper, cos_t, pltpu.roll(cos_t, IDX_HEAD, 1)),
                          jnp.where(upper, sin_t, pltpu.roll(sin_t, IDX_HEAD, 1)),
                          IDX_HEAD, IDX_HEAD // 8)
    for h in range(ATT_HEADS):
        sl = slice(h * LANES, (h + 1) * LANES)
        qo_ref[:, sl] = _rope(q_ref[:, sl], ca, sla, sha, ATT_HEAD // 8).astype(bf16)
        qio_ref[:, sl] = _rope(qi_ref[:, sl], ci, sli, shi, IDX_HEAD // 8).astype(bf16)
    for h in range(ATT_KV_HEADS):
        sl = slice(h * LANES, (h + 1) * LANES)
        ko_ref[:, sl] = _rope(ka_ref[:, sl], ca, sla, sha, ATT_HEAD // 8).astype(bf16)
    vo_ref[...] = va_ref[...].astype(bf16)
    kiwi = kiwi_ref[...]
    ki_only = jnp.where(lane < IDX_HEAD, kiwi, 0.0)
    ki = _rope(ki_only, ci, sli, shi, IDX_HEAD // 8)
    ki2_ref[...] = (ki + pltpu.roll(ki, IDX_HEAD, 1)).astype(bf16)
    wi = pltpu.roll(kiwi, LANES - IDX_HEAD, 1)
    wi_ref[...] = jnp.where(lane < IDX_HEADS, wi, 0.0) * (IDX_HEADS ** -0.5 * IDX_HEAD ** -0.5)


def _dsa_prep(p, pos, freq, tm):
    t = p.shape[0]
    blk = lambda width, col: pl.BlockSpec((tm, width), lambda i: (i, col // width))
    out = lambda width: pl.BlockSpec((tm, width), lambda i: (i, 0))
    vec = pl.BlockSpec((1, LANES), lambda i: (0, 0))
    kvw = ATT_KV_HEADS * ATT_HEAD
    return pl.pallas_call(
        _dsa_prep_kernel,
        grid=(t // tm,),
        in_specs=[pl.BlockSpec((tm, 1), lambda i: (i, 0)), vec,
                  blk(ATT_WIDTH, COL_Q), blk(IDX_HEADS * IDX_HEAD, COL_QI),
                  blk(kvw, COL_KATT), blk(kvw, COL_VATT), blk(LANES, COL_KIWI)],
        out_specs=[out(ATT_WIDTH), out(IDX_HEADS * IDX_HEAD), out(kvw), out(kvw), out(LANES), out(LANES)],
        out_shape=[jax.ShapeDtypeStruct((t, ATT_WIDTH), bf16),
                   jax.ShapeDtypeStruct((t, IDX_HEADS * IDX_HEAD), bf16),
                   jax.ShapeDtypeStruct((t, kvw), bf16),
                   jax.ShapeDtypeStruct((t, kvw), bf16),
                   jax.ShapeDtypeStruct((t, LANES), bf16),
                   jax.ShapeDtypeStruct((t, LANES), f32)],
        compiler_params=_params("parallel"),
        name="dsa_prep",
    )(pos, freq, p, p, p, p, p)


def _dsa_kernel(q_ref, qi_ref, wi_ref, k_ref, v_ref, ki2_ref, o_ref, score_ref, key_ref, *, n_sel, bucket_len):
    tq = q_ref.shape[0]
    seq = k_ref.shape[0]
    bucket = ((pl.program_id(1) + 1) * tq - 1) // bucket_len
    for j in range(seq // bucket_len):
        pl.when(bucket == j)(functools.partial(
            _dsa_body, q_ref, qi_ref, wi_ref, k_ref, v_ref, ki2_ref, o_ref, score_ref, key_ref,
            n_sel=n_sel, s_len=(j + 1) * bucket_len))


def _dsa_body(q_ref, qi_ref, wi_ref, k_ref, v_ref, ki2_ref, o_ref, score_ref, key_ref, *, n_sel, s_len):
    tq = q_ref.shape[0]
    q0 = pl.program_id(1) * tq
    lane = lax.broadcasted_iota(jnp.int32, (1, LANES), 1)
    first_half = lane < IDX_HEAD
    ki2 = ki2_ref[0:s_len, :]
    wi = wi_ref[...]

    score = jnp.zeros((tq, s_len), f32)
    for h in range(IDX_HEADS):
        pair = h // 2
        m = first_half if h % 2 == 0 else jnp.logical_not(first_half)
        qm = jnp.where(m, qi_ref[:, pair * LANES:(pair + 1) * LANES], jnp.zeros((), bf16))
        rel = jnp.maximum(_nt(qm, ki2), 0.0)
        w_h = jnp.sum(jnp.where(lane == h, wi, 0.0), axis=-1, keepdims=True)
        score = score + rel * w_h

    q_pos = q0 + lax.broadcasted_iota(jnp.int32, (tq, 1), 0)
    k_pos = lax.broadcasted_iota(jnp.int32, (1, s_len), 1)
    causal = k_pos <= q_pos
    score = jnp.where(causal, score, -1e30) + 0.0
    score_ref[:, 0:s_len] = score

    n8 = s_len // SUBLANES
    bits = pltpu.bitcast(score.T, jnp.int32)
    key_t = jnp.where(bits >= 0, bits, bits ^ jnp.int32(0x7FFFFFFF))
    key_ref[0:n8] = key_t.reshape(n8, SUBLANES, tq)
    int_min = jnp.int32(-2 ** 31)

    def search(i, t_u):
        cand = t_u | lax.shift_right_logical(int_min, i)
        ge = key_ref[0:n8] >= (cand ^ int_min)[None]
        ones = jnp.where(ge, 1.0, 0.0).reshape(n8 // COUNT_CHAINS, COUNT_CHAINS, SUBLANES, tq)
        cnt = jnp.sum(jnp.sum(ones, axis=0), axis=0)
        for shift in (4, 2, 1):
            cnt = cnt + pltpu.roll(cnt, shift, 0)
        return jnp.where(cnt >= n_sel, cand, t_u)

    t_u = lax.fori_loop(0, 32, search, jnp.zeros((SUBLANES, tq), jnp.int32), unroll=4)
    t_key = t_u[0:1, :] ^ int_min
    thr_row = pltpu.bitcast(jnp.where(t_key >= 0, t_key, t_key ^ jnp.int32(0x7FFFFFFF)), f32)
    diag = lax.broadcasted_iota(jnp.int32, (tq, tq), 0) == lax.broadcasted_iota(jnp.int32, (tq, tq), 1)
    thr_col = jnp.sum(jnp.where(diag, thr_row, 0.0), axis=-1, keepdims=True)
    selected = (score_ref[:, 0:s_len] >= thr_col) & causal

    group = ATT_HEADS // ATT_KV_HEADS
    sel_g = jnp.concatenate([selected] * group, axis=0)
    for kv in range(ATT_KV_HEADS):
        k_h = k_ref[0:s_len, kv * LANES:(kv + 1) * LANES]
        v_h = v_ref[0:s_len, kv * LANES:(kv + 1) * LANES]
        qg = jnp.concatenate([q_ref[:, (kv * group + g) * LANES:(kv * group + g + 1) * LANES]
                              for g in range(group)], axis=0)
        s = _nt(qg, k_h) * (ATT_HEAD ** -0.5)
        s = jnp.where(sel_g, s, -jnp.inf)
        s_max = jnp.max(s, axis=-1, keepdims=True)
        e = jnp.exp(s - s_max)
        denom = jnp.sum(e, axis=-1, keepdims=True)
        o = _mm(e.astype(bf16), v_h) / denom
        for g in range(group):
            hq = kv * group + g
            o_ref[:, hq * LANES:(hq + 1) * LANES] = o[g * tq:(g + 1) * tq].astype(o_ref.dtype)


def _dsa(q, qi, wi, k, v, ki2, batch, seq, tq):
    nq = seq // tq
    n_sel = min(TOPK_MAX, seq // 4)
    qblk = lambda width: pl.BlockSpec((tq, width), lambda b, i: (b * nq + i, 0))
    kblk = lambda width: pl.BlockSpec((seq, width), lambda b, i: (b, 0))
    kvw = ATT_KV_HEADS * ATT_HEAD
    bucket_len = max(tq, seq // DSA_BUCKETS)
    return pl.pallas_call(
        functools.partial(_dsa_kernel, n_sel=n_sel, bucket_len=bucket_len),
        grid=(batch, nq),
        in_specs=[qblk(ATT_WIDTH), qblk(IDX_HEADS * IDX_HEAD), qblk(LANES),
                  kblk(kvw), kblk(kvw), kblk(LANES)],
        out_specs=qblk(ATT_WIDTH),
        out_shape=jax.ShapeDtypeStruct((batch * seq, ATT_WIDTH), bf16),
        scratch_shapes=[pltpu.VMEM((tq, seq), f32),
                        pltpu.VMEM((seq // SUBLANES, SUBLANES, tq), jnp.int32)],
        compiler_params=_params("parallel", "arbitrary"),
        name="dsa_attention",
    )(q, qi, wi, k, v, ki2)


def _mix_out_kernel(x_ref, ya_ref, yb_ref, wa_ref, wb_ref, o_ref):
    o_ref[...] = x_ref[...] + _mm(ya_ref[...], wa_ref[...]) + _mm(yb_ref[...], wb_ref[...])


def _mix_out(x, ya, yb, w, tm, tn):
    t, d = x.shape
    ka = ya.shape[1]
    kb = yb.shape[1]
    return pl.pallas_call(
        _mix_out_kernel,
        grid=(t // tm, d // tn),
        in_specs=[pl.BlockSpec((tm, tn), lambda i, j: (i, j)),
                  pl.BlockSpec((tm, ka), lambda i, j: (i, 0)),
                  pl.BlockSpec((tm, kb), lambda i, j: (i, 0)),
                  pl.BlockSpec((ka, tn), lambda i, j: (0, j)),
                  pl.BlockSpec((kb, tn), lambda i, j: (ka // kb, j))],
        out_specs=pl.BlockSpec((tm, tn), lambda i, j: (i, j)),
        out_shape=jax.ShapeDtypeStruct((t, d), f32),
        compiler_params=_params("parallel", "arbitrary"),
        name="mix_out",
    )(x, ya, yb, w, w)


def _cross_kernel(h_ref, g_ref, kv_ref, wq_ref, wo_ref, o_ref):
    h = h_ref[...]
    hn = _rms(h, g_ref[...]).astype(bf16)
    q = _mm(hn, wq_ref[...]).astype(bf16)
    width = CROSS_HEADS * CROSS_HEAD
    outs = []
    for hd in range(CROSS_HEADS):
        sl = slice(hd * CROSS_HEAD, (hd + 1) * CROSS_HEAD)
        k_h = kv_ref[:, sl]
        v_h = kv_ref[:, width + hd * CROSS_HEAD: width + (hd + 1) * CROSS_HEAD]
        s = _nt(q[:, sl], k_h) * (CROSS_HEAD ** -0.5)
        e = jnp.exp(s - jnp.max(s, axis=-1, keepdims=True))
        outs.append((_mm(e.astype(bf16), v_h) / jnp.sum(e, axis=-1, keepdims=True)).astype(bf16))
    o = jnp.concatenate(outs, axis=-1)
    o_ref[...] = h + _mm(o, wo_ref[...])


def _cross(h, g, kv, wq, wo, batch, seq, n_mem, tm):
    d = h.shape[1]
    ns = seq // tm
    width = CROSS_HEADS * CROSS_HEAD
    return pl.pallas_call(
        _cross_kernel,
        grid=(batch, ns),
        in_specs=[pl.BlockSpec((tm, d), lambda b, i: (b * ns + i, 0)),
                  pl.BlockSpec((1, d), lambda b, i: (0, 0)),
                  pl.BlockSpec((n_mem, 2 * width), lambda b, i: (b, 0)),
                  pl.BlockSpec((d, width), lambda b, i: (0, 0)),
                  pl.BlockSpec((width, d), lambda b, i: (0, 0))],
        out_specs=pl.BlockSpec((tm, d), lambda b, i: (b * ns + i, 0)),
        out_shape=jax.ShapeDtypeStruct(h.shape, f32),
        compiler_params=_params("parallel", "arbitrary"),
        name="cross_attention",
    )(h, g, kv, wq, wo)


def _mlp_kernel(h_ref, g_ref, wu_ref, wd_ref, gf_ref, o_ref, hn_ref, *, final_norm):
    j = pl.program_id(1)

    @pl.when(j == 0)
    def _():
        h = h_ref[...]
        hn_ref[...] = _rms(h, g_ref[...]).astype(bf16)
        o_ref[...] = h

    u = jnp.maximum(_mm(hn_ref[...], wu_ref[...]), 0.0)
    o_ref[...] += _mm((u * u).astype(bf16), wd_ref[...])

    if final_norm:
        @pl.when(j == pl.num_programs(1) - 1)
        def _():
            o_ref[...] = _rms(o_ref[...], gf_ref[...])


def _mlp(h, g, wu, wd, gf, final_norm, tm, tf):
    t, d = h.shape
    dff = wu.shape[1]
    return pl.pallas_call(
        functools.partial(_mlp_kernel, final_norm=final_norm),
        grid=(t // tm, dff // tf),
        in_specs=[pl.BlockSpec((tm, d), lambda i, j: (i, 0)),
                  pl.BlockSpec((1, d), lambda i, j: (0, 0)),
                  pl.BlockSpec((d, tf), lambda i, j: (0, j)),
                  pl.BlockSpec((tf, d), lambda i, j: (j, 0)),
                  pl.BlockSpec((1, d), lambda i, j: (0, 0))],
        out_specs=pl.BlockSpec((tm, d), lambda i, j: (i, 0)),
        out_shape=jax.ShapeDtypeStruct((t, d), f32),
        scratch_shapes=[pltpu.VMEM((tm, d), bf16)],
        compiler_params=_params("parallel", "arbitrary"),
        name="mlp_final_norm",
    )(h, g, wu, wd, gf)


def _pack_in_proj(w_in, mu):
    d = w_in.shape[0]
    o = 0
    seg = {}
    for name, width in (("r", 1024), ("k", 1024), ("v", 1024), ("xw", 64), ("xa", 64), ("xg", 160),
                        ("q", 1024), ("katt", 256), ("vatt", 256), ("qi", 1024), ("ki", 64), ("wi", 16)):
        seg[name] = (o, o + width)
        o += width
    w_in = w_in.astype(bf16)
    col = lambda n: w_in[:, seg[n][0]:seg[n][1]]
    zeros = lambda n: jnp.zeros((d, n), w_in.dtype)
    w = jnp.concatenate([col("r"), col("k"), col("v"), col("q"), col("qi"), col("katt"), col("vatt"),
                         col("ki"), col("wi"), zeros(LANES - 80),
                         col("xw"), col("xa"), col("xg"), zeros(LORA_BLOCK - 288)], axis=1)
    mu_lo = jnp.concatenate([mu[3072:3360], jnp.zeros((LORA_BLOCK - 288,), mu.dtype)])
    mus = (mu[0:1024][None], mu[1024:2048][None], mu[2048:3072][None], mu_lo[None])
    return w, mus


def _rope_freq():
    def inv_freq(head):
        half = head // 8
        return ROPE_THETA ** (-jnp.arange(half, dtype=f32) / half)
    lane = jnp.arange(LANES)
    att = inv_freq(ATT_HEAD)[lane % (ATT_HEAD // 8)]
    idx = inv_freq(IDX_HEAD)[lane % (IDX_HEAD // 8)]
    in_idx = (lane >= IDX_HEAD) & (lane < IDX_HEAD + IDX_HEAD // 4)
    return jnp.where(lane < ATT_HEAD // 4, att, jnp.where(in_idx, idx, 0.0)).astype(f32)[None]


def kernel(x, mem, positions, norm_mix, w_in, rwkv_mu, w_decay0, w_decay_up, a0, a_up, g_up, k_k, k_a, r_k,
           lnx_w, lnx_b, w_mix_out, norm_cross, norm_mem, w_q_cross, w_kv_cross, w_o_cross, norm_mlp,
           w_up, w_down, norm_final):
    batch, seq, d = x.shape
    n_mem = mem.shape[1]
    t = batch * seq
    depth = w_in.shape[0]
    tm = min(512, t)
    h = x.reshape(t, d)
    pos = positions.reshape(t, 1).astype(f32)
    mem2 = mem.reshape(batch * n_mem, d)
    row = lambda vct: vct.reshape(1, -1)

    for l in range(depth):
        w_in_p, mus = _pack_in_proj(w_in[l], rwkv_mu[l])
        zpad = jnp.zeros((DECAY_LORA, RWKV_WIDTH), f32)
        w_wa = jnp.concatenate([jnp.concatenate([w_decay_up[l], zpad], axis=1),
                                jnp.concatenate([zpad, a_up[l]], axis=1)], axis=0).astype(bf16)
        g_up_p = jnp.concatenate([g_up[l], jnp.zeros((LORA_BLOCK - LANES - GATE_LORA, RWKV_WIDTH), f32)],
                                 axis=0).astype(bf16)

        p = _norm_matmul(h, row(norm_mix[l]), w_in_p, f32, min(1024, t), 1024)
        y_rwkv = _rwkv(p, batch, seq, mus, row(w_decay0[l]), row(a0[l]), row(k_k[l]), row(k_a[l]),
                       row(lnx_w[l]), row(lnx_b[l]), row(r_k[l]), w_wa, g_up_p)
        q, qi, k_att, v_att, ki2, wi = _dsa_prep(p, pos, _rope_freq(), tm)
        y_att = _dsa(q, qi, wi, k_att, v_att, ki2, batch, seq, min(DSA_QUERY_BLOCK, seq))
        h = _mix_out(h, y_rwkv, y_att, w_mix_out[l].astype(bf16), tm, d)

        kv = _norm_matmul(mem2, row(norm_mem[l]), w_kv_cross[l].astype(bf16), bf16, min(512, batch * n_mem), 1024)
        h = _cross(h, row(norm_cross[l]), kv, w_q_cross[l].astype(bf16), w_o_cross[l].astype(bf16),
                   batch, seq, n_mem, tm)
        h = _mlp(h, row(norm_mlp[l]), w_up[l].astype(bf16), w_down[l].astype(bf16), row(norm_final),
                 l == depth - 1, tm, 1024)
    return h.reshape(batch, seq, d)
```

```python
import functools

import jax
import jax.numpy as jnp
from jax import lax
from jax.experimental import pallas as pl
from jax.experimental.pallas import tpu as pltpu

f32 = jnp.float32
bf16 = jnp.bfloat16

D_MODEL = 2048
RWKV_WIDTH = 1024
RWKV_HEAD = 64
DECAY_LORA = 64
AAA_LORA = 64
GATE_LORA = 160
RWKV_LN_EPS = 64e-5
ATT_WIDTH = 1024
ATT_HEAD = 128
ATT_HEADS = 8
ATT_KV_HEADS = 2
IDX_HEADS = 16
IDX_HEAD = 64
TOPK_MAX = 256
ROPE_THETA = 500000.0
CROSS_HEADS = 4
CROSS_HEAD = 128
D_FF = 4 * D_MODEL
NORM_EPS = 1e-5

LANES = 128
SUBLANES = 8
VMEM_LIMIT_BYTES = 56 * 1024 * 1024

COL_R, COL_K, COL_V, COL_Q, COL_QI = 0, 1024, 2048, 3072, 4096
COL_KATT, COL_VATT, COL_KIWI, COL_LORA = 5120, 5376, 5632, 5760
LORA_BLOCK = 384
IN_COLS_PACKED = 6144
RWKV_CHUNK = 64
_EXP_NEG_HALF = 0.6065306597126334
RWKV_BATCH_GROUP = 2
DSA_QUERY_BLOCK = 128
COUNT_CHAINS = 8
DSA_BUCKETS = 8


def _nt(a, b):
    return lax.dot_general(a, b, (((1,), (1,)), ((), ())), preferred_element_type=f32)


def _mm(a, b):
    return jnp.dot(a, b, preferred_element_type=f32)


def _rms(x, g):
    ms = jnp.mean(x * x, axis=-1, keepdims=True)
    return x * lax.rsqrt(ms + NORM_EPS) * g


def _params(*sem):
    return pltpu.CompilerParams(dimension_semantics=sem, vmem_limit_bytes=VMEM_LIMIT_BYTES)


def _norm_matmul_kernel(x_ref, g_ref, w_ref, o_ref, xn_ref):
    @pl.when(pl.program_id(1) == 0)
    def _():
        xn_ref[...] = _rms(x_ref[...], g_ref[...]).astype(bf16)

    o_ref[...] = _mm(xn_ref[...], w_ref[...]).astype(o_ref.dtype)


def _norm_matmul(x, g, w, out_dtype, tm, tn):
    t, d = x.shape
    n = w.shape[1]
    return pl.pallas_call(
        _norm_matmul_kernel,
        grid=(t // tm, n // tn),
        in_specs=[
            pl.BlockSpec((tm, d), lambda i, j: (i, 0)),
            pl.BlockSpec((1, d), lambda i, j: (0, 0)),
            pl.BlockSpec((d, tn), lambda i, j: (0, j)),
        ],
        out_specs=pl.BlockSpec((tm, tn), lambda i, j: (i, j)),
        out_shape=jax.ShapeDtypeStruct((t, n), out_dtype),
        scratch_shapes=[pltpu.VMEM((tm, d), bf16)],
        compiler_params=_params("parallel", "arbitrary"),
        name="norm_matmul",
    )(x, g, w)


def _in_proj_kernel(x_ref, g_ref, w_ref, mu_ref, o_ref, xn_ref, last_ref, *, seq, shift_tiles):
    i, j = pl.program_id(0), pl.program_id(1)
    tm = x_ref.shape[0]

    @pl.when(j == 0)
    def _():
        xn_ref[...] = _rms(x_ref[...], g_ref[...]).astype(bf16)

    @pl.when((i == 0) & (j == 0))
    def _():
        last_ref[...] = jnp.zeros_like(last_ref)

    p = _mm(xn_ref[...], w_ref[...])
    needs_shift = functools.reduce(jnp.logical_or, [j == t for t in shift_tiles])

    @pl.when(needs_shift)
    def _():
        row = lax.broadcasted_iota(jnp.int32, (tm, 1), 0)
        prev = jnp.where(row == 0, last_ref[j], pltpu.roll(p, 1, 0))
        prev = jnp.where((i * tm + row) % seq == 0, 0.0, prev)
        last_ref[j] = p[tm - 1:tm, :]
        o_ref[...] = p + (prev - p) * mu_ref[...]

    @pl.when(jnp.logical_not(needs_shift))
    def _():
        o_ref[...] = p


def _in_proj(x, g, w, mu, seq, shift_tiles, tm, tn):
    t, d = x.shape
    n = w.shape[1]
    return pl.pallas_call(
        functools.partial(_in_proj_kernel, seq=seq, shift_tiles=shift_tiles),
        grid=(t // tm, n // tn),
        in_specs=[
            pl.BlockSpec((tm, d), lambda i, j: (i, 0)),
            pl.BlockSpec((1, d), lambda i, j: (0, 0)),
            pl.BlockSpec((d, tn), lambda i, j: (0, j)),
            pl.BlockSpec((1, tn), lambda i, j: (0, j)),
        ],
        out_specs=pl.BlockSpec((tm, tn), lambda i, j: (i, j)),
        out_shape=jax.ShapeDtypeStruct((t, n), f32),
        scratch_shapes=[pltpu.VMEM((tm, d), bf16), pltpu.VMEM((n // tn, 1, tn), f32)],
        compiler_params=_params("arbitrary", "arbitrary"),
        name="in_proj",
    )(x, g, w, mu)


def _seg_sum(x, first_half):
    s_a = jnp.sum(jnp.where(first_half, x, 0.0), axis=-1, keepdims=True)
    s_b = jnp.sum(jnp.where(first_half, 0.0, x), axis=-1, keepdims=True)
    return jnp.where(first_half, s_a, s_b)


def _rwkv_kernel(r_ref, k_ref, v_ref, lo_ref,
                 w0_ref, a0_ref, kkw_ref, kaw_ref, lnw_ref, lnb_ref, rk_ref, wwa_ref, gup_ref,
                 y_ref, st_ref):
    G, C = r_ref.shape[0], r_ref.shape[1]
    n_pairs = RWKV_WIDTH // LANES

    @pl.when(pl.program_id(1) == 0)
    def _():
        st_ref[...] = jnp.zeros_like(st_ref)

    lane = lax.broadcasted_iota(jnp.int32, (1, LANES), 1)
    first_half = lane < RWKV_HEAD
    ti = lax.broadcasted_iota(jnp.int32, (C, C), 0)
    tj = lax.broadcasted_iota(jnp.int32, (C, C), 1)
    tri_b = jnp.where(ti >= tj, 1.0, 0.0).astype(bf16)
    t2 = lax.broadcasted_iota(jnp.int32, (C, 2 * C), 0)
    j2 = lax.broadcasted_iota(jnp.int32, (C, 2 * C), 1) % C
    tri2_strict = t2 > j2
    eye2 = jnp.where(t2 == j2, 1.0, 0.0)
    t4 = lax.broadcasted_iota(jnp.int32, (C, 4 * C), 0)
    j4 = lax.broadcasted_iota(jnp.int32, (C, 4 * C), 1) % C
    tri4_incl = t4 >= j4
    vi = lax.broadcasted_iota(jnp.int32, (LANES, LANES), 0)
    vj = lax.broadcasted_iota(jnp.int32, (LANES, LANES), 1)
    same_head = (vi < RWKV_HEAD) == (vj < RWKV_HEAD)

    def head_rows(x):
        xb = x.astype(bf16)
        zero = jnp.zeros((), bf16)
        return jnp.concatenate([jnp.where(first_half, xb, zero), jnp.where(first_half, zero, xb)], axis=0)

    units = [(g, p) for g in range(G) for p in range(n_pairs)]
    lanes_of = lambda p: slice(p * LANES, (p + 1) * LANES)
    v_all, gate_all, rkk_all = [], [], []
    a_t, b_t, k_t, r_t, g_end, s0 = [], [], [], [], [], []
    for g in range(G):
        r, k, v, lo = r_ref[g], k_ref[g], v_ref[g], lo_ref[g]
        wa_in = lo[:, :LANES]
        wa_in = jnp.where(first_half, 1.0 - 2.0 / (1.0 + jnp.exp(2.0 * wa_in)), wa_in).astype(bf16)
        wa = _mm(wa_in, wwa_ref[...])
        logdecay = -_EXP_NEG_HALF / (1.0 + jnp.exp(-(w0_ref[...] + wa[:, :RWKV_WIDTH])))
        a = 1.0 / (1.0 + jnp.exp(-(a0_ref[...] + wa[:, RWKV_WIDTH:])))
        gate_all.append(_mm((1.0 / (1.0 + jnp.exp(-lo[:, LANES:]))).astype(bf16), gup_ref[...]))
        ld_hi = logdecay.astype(bf16)
        ld_r1 = logdecay - ld_hi.astype(f32)
        ld_mid = ld_r1.astype(bf16)
        ld_lo = (ld_r1 - ld_mid.astype(f32)).astype(bf16)
        cum = _mm(tri_b, ld_hi) + _mm(tri_b, ld_mid) + _mm(tri_b, ld_lo)
        kk = k * kkw_ref[...]
        k2 = k * (1.0 + (a - 1.0) * kaw_ref[...])
        rkk_all.append(r * k2 * rk_ref[...])
        v_all.append(v)
        ecum = jnp.exp(cum)
        encum = jnp.exp(-cum)
        ecum_prev = jnp.exp(cum - logdecay)
        for p in range(n_pairs):
            sl = lanes_of(p)
            kkp = kk[:, sl]
            kkn = kkp * lax.rsqrt(jnp.maximum(_seg_sum(kkp * kkp, first_half), 1e-24))
            a_t.append(-kkn * ecum_prev[:, sl])
            b_t.append(kkn * a[:, sl] * encum[:, sl])
            k_t.append(k2[:, sl] * encum[:, sl])
            r_t.append(r[:, sl] * ecum[:, sl])
            g_end.append(ecum[C - 1:C, sl])
            s0.append(st_ref[g * n_pairs + p])

    n = len(units)
    vp = [v_all[g][:, lanes_of(p)] for g, p in units]
    v_rows = [head_rows(x) for x in vp]
    ar = [jnp.concatenate([a_t[i], r_t[i]], axis=0).astype(bf16) for i in range(n)]
    ar0 = [_nt(ar[i], s0[i].astype(bf16)) for i in range(n)]
    gram = [_nt(ar[i], jnp.concatenate([head_rows(b_t[i]), head_rows(k_t[i])], axis=0)) for i in range(n)]
    l_b = [jnp.where(tri2_strict, x[:C, :2 * C], 0.0) for x in gram]
    l_k = [jnp.where(tri2_strict, x[:C, 2 * C:], 0.0).astype(bf16) for x in gram]
    m_bk = [jnp.where(tri4_incl, x[C:, :], 0.0).astype(bf16) for x in gram]
    w = [ar0[i][:C] + _mm(l_k[i], v_rows[i]) for i in range(n)]
    inv = [eye2 + x for x in l_b]
    l_pow = l_b
    l_pow_rows = [head_rows(x) for x in l_pow]
    for _ in range(C.bit_length() - 2):
        l_pow = [_mm(x.astype(bf16), y) for x, y in zip(l_pow, l_pow_rows)]
        l_pow_rows = [head_rows(x) for x in l_pow]
        inv = [x + _mm(x.astype(bf16), y) for x, y in zip(inv, l_pow_rows)]
    u = [_mm(inv[i].astype(bf16), head_rows(w[i])) for i in range(n)]
    y_all = [ar0[i][C:] + _mm(m_bk[i], jnp.concatenate([head_rows(u[i]), v_rows[i]], axis=0))
             for i in range(n)]
    upd = [_mm(jnp.concatenate([u[i], vp[i]], axis=0).T.astype(bf16),
               (jnp.concatenate([b_t[i], k_t[i]], axis=0) * g_end[i]).astype(bf16)) for i in range(n)]
    for i, (g, p) in enumerate(units):
        sl = lanes_of(p)
        st_ref[g * n_pairs + p] = s0[i] * g_end[i] + jnp.where(same_head, upd[i], 0.0)
        y = y_all[i]
        mean = _seg_sum(y, first_half) * (1.0 / RWKV_HEAD)
        yc = y - mean
        var = _seg_sum(yc * yc, first_half) * (1.0 / RWKV_HEAD)
        yn = yc * lax.rsqrt(var + RWKV_LN_EPS) * lnw_ref[:, sl] + lnb_ref[:, sl]
        bonus = _seg_sum(rkk_all[g][:, sl], first_half) * vp[i]
        y_ref[g, :, sl] = ((yn + bonus) * gate_all[g][:, sl]).astype(y_ref.dtype)


def _rwkv(p, batch, seq, w0, a0, k_k, k_a, lnx_w, lnx_b, r_k, w_wa, g_up_p):
    C = RWKV_CHUNK
    G = RWKV_BATCH_GROUP if batch % RWKV_BATCH_GROUP == 0 else 1
    W = RWKV_WIDTH
    p3 = p.reshape(batch, seq, p.shape[-1])
    row_block = lambda width, col: pl.BlockSpec((G, C, width), lambda b, c: (b, c, col // width))
    vec = lambda width: pl.BlockSpec((1, width), lambda b, c: (0, 0))
    full = lambda arr: pl.BlockSpec(arr.shape, lambda b, c: (0, 0))
    y = pl.pallas_call(
        _rwkv_kernel,
        grid=(batch // G, seq // C),
        in_specs=[row_block(W, COL_R), row_block(W, COL_K), row_block(W, COL_V),
                  row_block(LORA_BLOCK, COL_LORA),
                  vec(W), vec(W), vec(W), vec(W), vec(W), vec(W), vec(W),
                  full(w_wa), full(g_up_p)],
        out_specs=pl.BlockSpec((G, C, W), lambda b, c: (b, c, 0)),
        out_shape=jax.ShapeDtypeStruct((batch, seq, W), bf16),
        scratch_shapes=[pltpu.VMEM((G * W // LANES, LANES, LANES), f32)],
        compiler_params=_params("parallel", "arbitrary"),
        name="rwkv7_chunked",
    )(p3, p3, p3, p3, w0, a0, k_k, k_a, lnx_w, lnx_b, r_k, w_wa, g_up_p)
    return y.reshape(batch * seq, W)


def _rope(x, cos, sin_lo, sin_hi, half):
    n = x.shape[-1]
    return x * cos + pltpu.roll(x, n - half, 1) * sin_lo + pltpu.roll(x, half, 1) * sin_hi


def _dsa_prep_kernel(pos_ref, freq_ref, q_ref, qi_ref, ka_ref, va_ref, kiwi_ref,
                     qo_ref, qio_ref, ko_ref, vo_ref, ki2_ref, wi_ref):
    pos = pos_ref[...]
    lane = lax.broadcasted_iota(jnp.int32, (1, LANES), 1)
    ang = pos * freq_ref[...]
    cos_t, sin_t = jnp.cos(ang), jnp.sin(ang)

    def tables(cos, sin, head, half):
        in_head = lane % head
        sin_lo = jnp.where(in_head < half, -sin, 0.0)
        sin_hi = jnp.where((in_head >= half) & (in_head < 2 * half), sin, 0.0)
        return cos, sin_lo, sin_hi

    att_lanes = lane < ATT_HEAD // 4
    upper = lane >= IDX_HEAD
    ca, sla, sha = tables(jnp.where(att_lanes, cos_t, 1.0), jnp.where(att_lanes, sin_t, 0.0),
                          ATT_HEAD, ATT_HEAD // 8)
    ci, sli, shi = tables(jnp.where(upper, cos_t, pltpu.roll(cos_t, IDX_HEAD, 1)),
                          jnp.where(upper, sin_t, pltpu.roll(sin_t, IDX_HEAD, 1)),
                          IDX_HEAD, IDX_HEAD // 8)
    for h in range(ATT_HEADS):
        sl = slice(h * LANES, (h + 1) * LANES)
        qo_ref[:, sl] = _rope(q_ref[:, sl], ca, sla, sha, ATT_HEAD // 8).astype(bf16)
        qio_ref[:, sl] = _rope(qi_ref[:, sl], ci, sli, shi, IDX_HEAD // 8).astype(bf16)
    for h in range(ATT_KV_HEADS):
        sl = slice(h * LANES, (h + 1) * LANES)
        ko_ref[:, sl] = _rope(ka_ref[:, sl], ca, sla, sha, ATT_HEAD // 8).astype(bf16)
    vo_ref[...] = va_ref[...].astype(bf16)
    kiwi = kiwi_ref[...]
    ki_only = jnp.where(lane < IDX_HEAD, kiwi, 0.0)
    ki = _rope(ki_only, ci, sli, shi, IDX_HEAD // 8)
    ki2_ref[...] = (ki + pltpu.roll(ki, IDX_HEAD, 1)).astype(bf16)
    wi = pltpu.roll(kiwi, LANES - IDX_HEAD, 1)
    wi_ref[...] = jnp.where(lane < IDX_HEADS, wi, 0.0) * (IDX_HEADS ** -0.5 * IDX_HEAD ** -0.5)


def _dsa_prep(p, pos, freq, tm):
    t = p.shape[0]
    blk = lambda width, col: pl.BlockSpec((tm, width), lambda i: (i, col // width))
    out = lambda width: pl.BlockSpec((tm, width), lambda i: (i, 0))
    vec = pl.BlockSpec((1, LANES), lambda i: (0, 0))
    kvw = ATT_KV_HEADS * ATT_HEAD
    return pl.pallas_call(
        _dsa_prep_kernel,
        grid=(t // tm,),
        in_specs=[pl.BlockSpec((tm, 1), lambda i: (i, 0)), vec,
                  blk(ATT_WIDTH, COL_Q), blk(IDX_HEADS * IDX_HEAD, COL_QI),
                  blk(kvw, COL_KATT), blk(kvw, COL_VATT), blk(LANES, COL_KIWI)],
        out_specs=[out(ATT_WIDTH), out(IDX_HEADS * IDX_HEAD), out(kvw), out(kvw), out(LANES), out(LANES)],
        out_shape=[jax.ShapeDtypeStruct((t, ATT_WIDTH), bf16),
                   jax.ShapeDtypeStruct((t, IDX_HEADS * IDX_HEAD), bf16),
                   jax.ShapeDtypeStruct((t, kvw), bf16),
                   jax.ShapeDtypeStruct((t, kvw), bf16),
                   jax.ShapeDtypeStruct((t, LANES), bf16),
                   jax.ShapeDtypeStruct((t, LANES), f32)],
        compiler_params=_params("parallel"),
        name="dsa_prep",
    )(pos, freq, p, p, p, p, p)


def _dsa_kernel(q_ref, qi_ref, wi_ref, k_ref, v_ref, ki2_ref, o_ref, score_ref, key_ref, *, n_sel, bucket_len):
    tq = q_ref.shape[0]
    seq = k_ref.shape[0]
    bucket = ((pl.program_id(1) + 1) * tq - 1) // bucket_len
    for j in range(seq // bucket_len):
        pl.when(bucket == j)(functools.partial(
            _dsa_body, q_ref, qi_ref, wi_ref, k_ref, v_ref, ki2_ref, o_ref, score_ref, key_ref,
            n_sel=n_sel, s_len=(j + 1) * bucket_len))


def _dsa_body(q_ref, qi_ref, wi_ref, k_ref, v_ref, ki2_ref, o_ref, score_ref, key_ref, *, n_sel, s_len):
    tq = q_ref.shape[0]
    q0 = pl.program_id(1) * tq
    lane = lax.broadcasted_iota(jnp.int32, (1, LANES), 1)
    first_half = lane < IDX_HEAD
    ki2 = ki2_ref[0:s_len, :]
    wi = wi_ref[...]

    score = jnp.zeros((tq, s_len), f32)
    for h in range(IDX_HEADS):
        pair = h // 2
        m = first_half if h % 2 == 0 else jnp.logical_not(first_half)
        qm = jnp.where(m, qi_ref[:, pair * LANES:(pair + 1) * LANES], jnp.zeros((), bf16))
        rel = jnp.maximum(_nt(qm, ki2), 0.0)
        w_h = jnp.sum(jnp.where(lane == h, wi, 0.0), axis=-1, keepdims=True)
        score = score + rel * w_h

    q_pos = q0 + lax.broadcasted_iota(jnp.int32, (tq, 1), 0)
    k_pos = lax.broadcasted_iota(jnp.int32, (1, s_len), 1)
    causal = k_pos <= q_pos
    score = jnp.where(causal, score, -1e30) + 0.0
    score_ref[:, 0:s_len] = score

    n8 = s_len // SUBLANES
    bits = pltpu.bitcast(score.T, jnp.int32)
    key_t = jnp.where(bits >= 0, bits, bits ^ jnp.int32(0x7FFFFFFF))
    key_ref[0:n8] = key_t.reshape(n8, SUBLANES, tq)
    int_min = jnp.int32(-2 ** 31)

    def search(i, t_u):
        cand = t_u | lax.shift_right_logical(int_min, i)
        ge = key_ref[0:n8] >= (cand ^ int_min)[None]
        ones = jnp.where(ge, 1.0, 0.0).reshape(n8 // COUNT_CHAINS, COUNT_CHAINS, SUBLANES, tq)
        cnt = jnp.sum(jnp.sum(ones, axis=0), axis=0)
        for shift in (4, 2, 1):
            cnt = cnt + pltpu.roll(cnt, shift, 0)
        return jnp.where(cnt >= n_sel, cand, t_u)

    t_u = lax.fori_loop(0, 32, search, jnp.zeros((SUBLANES, tq), jnp.int32), unroll=4)
    t_key = t_u[0:1, :] ^ int_min
    thr_row = pltpu.bitcast(jnp.where(t_key >= 0, t_key, t_key ^ jnp.int32(0x7FFFFFFF)), f32)
    diag = lax.broadcasted_iota(jnp.int32, (tq, tq), 0) == lax.broadcasted_iota(jnp.int32, (tq, tq), 1)
    thr_col = jnp.sum(jnp.where(diag, thr_row, 0.0), axis=-1, keepdims=True)
    selected = (score_ref[:, 0:s_len] >= thr_col) & causal

    group = ATT_HEADS // ATT_KV_HEADS
    sel_g = jnp.concatenate([selected] * group, axis=0)
    for kv in range(ATT_KV_HEADS):
        k_h = k_ref[0:s_len, kv * LANES:(kv + 1) * LANES]
        v_h = v_ref[0:s_len, kv * LANES:(kv + 1) * LANES]
        qg = jnp.concatenate([q_ref[:, (kv * group + g) * LANES:(kv * group + g + 1) * LANES]
                              for g in range(group)], axis=0)
        s = _nt(qg, k_h) * (ATT_HEAD ** -0.5)
        s = jnp.where(sel_g, s, -jnp.inf)
        s_max = jnp.max(s, axis=-1, keepdims=True)
        e = jnp.exp(s - s_max)
        denom = jnp.sum(e, axis=-1, keepdims=True)
        o = _mm(e.astype(bf16), v_h) / denom
        for g in range(group):
            hq = kv * group + g
            o_ref[:, hq * LANES:(hq + 1) * LANES] = o[g * tq:(g + 1) * tq].astype(o_ref.dtype)


def _dsa(q, qi, wi, k, v, ki2, batch, seq, tq):
    nq = seq // tq
    n_sel = min(TOPK_MAX, seq // 4)
    qblk = lambda width: pl.BlockSpec((tq, width), lambda b, i: (b * nq + i, 0))
    kblk = lambda width: pl.BlockSpec((seq, width), lambda b, i: (b, 0))
    kvw = ATT_KV_HEADS * ATT_HEAD
    bucket_len = max(tq, seq // DSA_BUCKETS)
    return pl.pallas_call(
        functools.partial(_dsa_kernel, n_sel=n_sel, bucket_len=bucket_len),
        grid=(batch, nq),
        in_specs=[qblk(ATT_WIDTH), qblk(IDX_HEADS * IDX_HEAD), qblk(LANES),
                  kblk(kvw), kblk(kvw), kblk(LANES)],
        out_specs=qblk(ATT_WIDTH),
        out_shape=jax.ShapeDtypeStruct((batch * seq, ATT_WIDTH), bf16),
        scratch_shapes=[pltpu.VMEM((tq, seq), f32),
                        pltpu.VMEM((seq // SUBLANES, SUBLANES, tq), jnp.int32)],
        compiler_params=_params("parallel", "arbitrary"),
        name="dsa_attention",
    )(q, qi, wi, k, v, ki2)


def _mix_out_kernel(x_ref, ya_ref, yb_ref, wa_ref, wb_ref, o_ref):
    o_ref[...] = x_ref[...] + _mm(ya_ref[...], wa_ref[...]) + _mm(yb_ref[...], wb_ref[...])


def _mix_out(x, ya, yb, w, tm, tn):
    t, d = x.shape
    ka = ya.shape[1]
    kb = yb.shape[1]
    return pl.pallas_call(
        _mix_out_kernel,
        grid=(t // tm, d // tn),
        in_specs=[pl.BlockSpec((tm, tn), lambda i, j: (i, j)),
                  pl.BlockSpec((tm, ka), lambda i, j: (i, 0)),
                  pl.BlockSpec((tm, kb), lambda i, j: (i, 0)),
                  pl.BlockSpec((ka, tn), lambda i, j: (0, j)),
                  pl.BlockSpec((kb, tn), lambda i, j: (ka // kb, j))],
        out_specs=pl.BlockSpec((tm, tn), lambda i, j: (i, j)),
        out_shape=jax.ShapeDtypeStruct((t, d), f32),
        compiler_params=_params("parallel", "arbitrary"),
        name="mix_out",
    )(x, ya, yb, w, w)


def _cross_kernel(h_ref, g_ref, kv_ref, wq_ref, wo_ref, o_ref):
    h = h_ref[...]
    hn = _rms(h, g_ref[...]).astype(bf16)
    q = _mm(hn, wq_ref[...]).astype(bf16)
    width = CROSS_HEADS * CROSS_HEAD
    outs = []
    for hd in range(CROSS_HEADS):
        sl = slice(hd * CROSS_HEAD, (hd + 1) * CROSS_HEAD)
        k_h = kv_ref[:, sl]
        v_h = kv_ref[:, width + hd * CROSS_HEAD: width + (hd + 1) * CROSS_HEAD]
        s = _nt(q[:, sl], k_h) * (CROSS_HEAD ** -0.5)
        e = jnp.exp(s - jnp.max(s, axis=-1, keepdims=True))
        outs.append((_mm(e.astype(bf16), v_h) / jnp.sum(e, axis=-1, keepdims=True)).astype(bf16))
    o = jnp.concatenate(outs, axis=-1)
    o_ref[...] = h + _mm(o, wo_ref[...])


def _cross(h, g, kv, wq, wo, batch, seq, n_mem, tm):
    d = h.shape[1]
    ns = seq // tm
    width = CROSS_HEADS * CROSS_HEAD
    return pl.pallas_call(
        _cross_kernel,
        grid=(batch, ns),
        in_specs=[pl.BlockSpec((tm, d), lambda b, i: (b * ns + i, 0)),
                  pl.BlockSpec((1, d), lambda b, i: (0, 0)),
                  pl.BlockSpec((n_mem, 2 * width), lambda b, i: (b, 0)),
                  pl.BlockSpec((d, width), lambda b, i: (0, 0)),
                  pl.BlockSpec((width, d), lambda b, i: (0, 0))],
        out_specs=pl.BlockSpec((tm, d), lambda b, i: (b * ns + i, 0)),
        out_shape=jax.ShapeDtypeStruct(h.shape, f32),
        compiler_params=_params("parallel", "arbitrary"),
        name="cross_attention",
    )(h, g, kv, wq, wo)


def _mlp_kernel(h_ref, g_ref, wu_ref, wd_ref, gf_ref, o_ref, hn_ref, *, final_norm):
    j = pl.program_id(1)

    @pl.when(j == 0)
    def _():
        h = h_ref[...]
        hn_ref[...] = _rms(h, g_ref[...]).astype(bf16)
        o_ref[...] = h

    u = jnp.maximum(_mm(hn_ref[...], wu_ref[...]), 0.0)
    o_ref[...] += _mm((u * u).astype(bf16), wd_ref[...])

    if final_norm:
        @pl.when(j == pl.num_programs(1) - 1)
        def _():
            o_ref[...] = _rms(o_ref[...], gf_ref[...])


def _mlp(h, g, wu, wd, gf, final_norm, tm, tf):
    t, d = h.shape
    dff = wu.shape[1]
    return pl.pallas_call(
        functools.partial(_mlp_kernel, final_norm=final_norm),
        grid=(t // tm, dff // tf),
        in_specs=[pl.BlockSpec((tm, d), lambda i, j: (i, 0)),
                  pl.BlockSpec((1, d), lambda i, j: (0, 0)),
                  pl.BlockSpec((d, tf), lambda i, j: (0, j)),
                  pl.BlockSpec((tf, d), lambda i, j: (j, 0)),
                  pl.BlockSpec((1, d), lambda i, j: (0, 0))],
        out_specs=pl.BlockSpec((tm, d), lambda i, j: (i, 0)),
        out_shape=jax.ShapeDtypeStruct((t, d), f32),
        scratch_shapes=[pltpu.VMEM((tm, d), bf16)],
        compiler_params=_params("parallel", "arbitrary"),
        name="mlp_final_norm",
    )(h, g, wu, wd, gf)


def _pack_in_proj(w_in, mu):
    d = w_in.shape[0]
    o = 0
    seg = {}
    for name, width in (("r", 1024), ("k", 1024), ("v", 1024), ("xw", 64), ("xa", 64), ("xg", 160),
                        ("q", 1024), ("katt", 256), ("vatt", 256), ("qi", 1024), ("ki", 64), ("wi", 16)):
        seg[name] = (o, o + width)
        o += width
    w_in = w_in.astype(bf16)
    col = lambda n: w_in[:, seg[n][0]:seg[n][1]]
    zeros = lambda n: jnp.zeros((d, n), w_in.dtype)
    w = jnp.concatenate([col("r"), col("k"), col("v"), col("q"), col("qi"), col("katt"), col("vatt"),
                         col("ki"), col("wi"), zeros(LANES - 80),
                         col("xw"), col("xa"), col("xg"), zeros(LORA_BLOCK - 288)], axis=1)
    mu_p = jnp.concatenate([mu[0:3072], jnp.zeros((COL_LORA - 3072,), mu.dtype), mu[3072:3360],
                            jnp.zeros((LORA_BLOCK - 288,), mu.dtype)])[None]
    return w, mu_p


def _rope_freq():
    def inv_freq(head):
        half = head // 8
        return ROPE_THETA ** (-jnp.arange(half, dtype=f32) / half)
    lane = jnp.arange(LANES)
    att = inv_freq(ATT_HEAD)[lane % (ATT_HEAD // 8)]
    idx = inv_freq(IDX_HEAD)[lane % (IDX_HEAD // 8)]
    in_idx = (lane >= IDX_HEAD) & (lane < IDX_HEAD + IDX_HEAD // 4)
    return jnp.where(lane < ATT_HEAD // 4, att, jnp.where(in_idx, idx, 0.0)).astype(f32)[None]


def kernel(x, mem, positions, norm_mix, w_in, rwkv_mu, w_decay0, w_decay_up, a0, a_up, g_up, k_k, k_a, r_k,
           lnx_w, lnx_b, w_mix_out, norm_cross, norm_mem, w_q_cross, w_kv_cross, w_o_cross, norm_mlp,
           w_up, w_down, norm_final):
    batch, seq, d = x.shape
    n_mem = mem.shape[1]
    t = batch * seq
    depth = w_in.shape[0]
    tm = min(512, t)
    h = x.reshape(t, d)
    pos = positions.reshape(t, 1).astype(f32)
    mem2 = mem.reshape(batch * n_mem, d)
    row = lambda vct: vct.reshape(1, -1)

    for l in range(depth):
        w_in_p, mu_p = _pack_in_proj(w_in[l], rwkv_mu[l])
        zpad = jnp.zeros((DECAY_LORA, RWKV_WIDTH), f32)
        w_wa = jnp.concatenate([jnp.concatenate([w_decay_up[l], zpad], axis=1),
                                jnp.concatenate([zpad, a_up[l]], axis=1)], axis=0).astype(bf16)
        g_up_p = jnp.concatenate([g_up[l], jnp.zeros((LORA_BLOCK - LANES - GATE_LORA, RWKV_WIDTH), f32)],
                                 axis=0).astype(bf16)

        tn = 1024
        shift_tiles = tuple(sorted({c // tn for c in (COL_R, COL_K, COL_V, COL_LORA)}))
        p = _in_proj(h, row(norm_mix[l]), w_in_p, mu_p, seq, shift_tiles, min(1024, t), tn)
        y_rwkv = _rwkv(p, batch, seq, row(w_decay0[l]), row(a0[l]), row(k_k[l]), row(k_a[l]),
                       row(lnx_w[l]), row(lnx_b[l]), row(r_k[l]), w_wa, g_up_p)
        q, qi, k_att, v_att, ki2, wi = _dsa_prep(p, pos, _rope_freq(), tm)
        y_att = _dsa(q, qi, wi, k_att, v_att, ki2, batch, seq, min(DSA_QUERY_BLOCK, seq))
        h = _mix_out(h, y_rwkv, y_att, w_mix_out[l].astype(bf16), tm, d)

        kv = _norm_matmul(mem2, row(norm_mem[l]), w_kv_cross[l].astype(bf16), bf16, min(512, batch * n_mem), 1024)
        h = _cross(h, row(norm_cross[l]), kv, w_q_cross[l].astype(bf16), w_o_cross[l].astype(bf16),
                   batch, seq, n_mem, tm)
        h = _mlp(h, row(norm_mlp[l]), w_up[l].astype(bf16), w_down[l].astype(bf16), row(norm_final),
                 l == depth - 1, tm, 1024)
    return h.reshape(batch, seq, d)
```

```python
import functools

import jax
import jax.numpy as jnp
from jax import lax
from jax.experimental import pallas as pl
from jax.experimental.pallas import tpu as pltpu

f32 = jnp.float32
bf16 = jnp.bfloat16

D_MODEL = 2048
RWKV_WIDTH = 1024
RWKV_HEAD = 64
DECAY_LORA = 64
AAA_LORA = 64
GATE_LORA = 160
RWKV_LN_EPS = 64e-5
ATT_WIDTH = 1024
ATT_HEAD = 128
ATT_HEADS = 8
ATT_KV_HEADS = 2
IDX_HEADS = 16
IDX_HEAD = 64
TOPK_MAX = 256
ROPE_THETA = 500000.0
CROSS_HEADS = 4
CROSS_HEAD = 128
D_FF = 4 * D_MODEL
NORM_EPS = 1e-5

LANES = 128
SUBLANES = 8
VMEM_LIMIT_BYTES = 56 * 1024 * 1024

COL_R, COL_K, COL_V, COL_Q, COL_QI = 0, 1024, 2048, 3072, 4096
COL_KATT, COL_VATT, COL_KIWI, COL_LORA = 5120, 5376, 5632, 5760
LORA_BLOCK = 384
IN_COLS_PACKED = 6144
RWKV_CHUNK = 64
_EXP_NEG_HALF = 0.6065306597126334
RWKV_BATCH_GROUP = 2
DSA_QUERY_BLOCK = 128
COUNT_CHAINS = 8
DSA_BUCKETS = 8


def _nt(a, b):
    return lax.dot_general(a, b, (((1,), (1,)), ((), ())), preferred_element_type=f32)


def _mm(a, b):
    return jnp.dot(a, b, preferred_element_type=f32)


def _rms(x, g):
    ms = jnp.mean(x * x, axis=-1, keepdims=True)
    return x * lax.rsqrt(ms + NORM_EPS) * g


def _params(*sem):
    return pltpu.CompilerParams(dimension_semantics=sem, vmem_limit_bytes=VMEM_LIMIT_BYTES)


def _norm_matmul_kernel(x_ref, g_ref, w_ref, o_ref, xn_ref):
    @pl.when(pl.program_id(1) == 0)
    def _():
        xn_ref[...] = _rms(x_ref[...], g_ref[...]).astype(bf16)

    o_ref[...] = _mm(xn_ref[...], w_ref[...]).astype(o_ref.dtype)


def _norm_matmul(x, g, w, out_dtype, tm, tn):
    t, d = x.shape
    n = w.shape[1]
    return pl.pallas_call(
        _norm_matmul_kernel,
        grid=(t // tm, n // tn),
        in_specs=[
            pl.BlockSpec((tm, d), lambda i, j: (i, 0)),
            pl.BlockSpec((1, d), lambda i, j: (0, 0)),
            pl.BlockSpec((d, tn), lambda i, j: (0, j)),
        ],
        out_specs=pl.BlockSpec((tm, tn), lambda i, j: (i, j)),
        out_shape=jax.ShapeDtypeStruct((t, n), out_dtype),
        scratch_shapes=[pltpu.VMEM((tm, d), bf16)],
        compiler_params=_params("parallel", "arbitrary"),
        name="norm_matmul",
    )(x, g, w)


def _seg_sum(x, first_half):
    s_a = jnp.sum(jnp.where(first_half, x, 0.0), axis=-1, keepdims=True)
    s_b = jnp.sum(jnp.where(first_half, 0.0, x), axis=-1, keepdims=True)
    return jnp.where(first_half, s_a, s_b)


def _rwkv_kernel(r_ref, k_ref, v_ref, lo_ref, mur_ref, muk_ref, muv_ref, mulo_ref,
                 w0_ref, a0_ref, kkw_ref, kaw_ref, lnw_ref, lnb_ref, rk_ref, wwa_ref, gup_ref,
                 y_ref, st_ref, pr_ref, pk_ref, pv_ref, plo_ref):
    G, C = r_ref.shape[0], r_ref.shape[1]
    n_pairs = RWKV_WIDTH // LANES

    @pl.when(pl.program_id(1) == 0)
    def _():
        st_ref[...] = jnp.zeros_like(st_ref)
        pr_ref[...] = jnp.zeros_like(pr_ref)
        pk_ref[...] = jnp.zeros_like(pk_ref)
        pv_ref[...] = jnp.zeros_like(pv_ref)
        plo_ref[...] = jnp.zeros_like(plo_ref)

    row = lax.broadcasted_iota(jnp.int32, (C, 1), 0)
    lane = lax.broadcasted_iota(jnp.int32, (1, LANES), 1)
    first_half = lane < RWKV_HEAD
    ti = lax.broadcasted_iota(jnp.int32, (C, C), 0)
    tj = lax.broadcasted_iota(jnp.int32, (C, C), 1)
    tri_b = jnp.where(ti >= tj, 1.0, 0.0).astype(bf16)
    t2 = lax.broadcasted_iota(jnp.int32, (C, 2 * C), 0)
    j2 = lax.broadcasted_iota(jnp.int32, (C, 2 * C), 1) % C
    tri2_strict = t2 > j2
    eye2 = jnp.where(t2 == j2, 1.0, 0.0)
    t4 = lax.broadcasted_iota(jnp.int32, (C, 4 * C), 0)
    j4 = lax.broadcasted_iota(jnp.int32, (C, 4 * C), 1) % C
    tri4_incl = t4 >= j4
    vi = lax.broadcasted_iota(jnp.int32, (LANES, LANES), 0)
    vj = lax.broadcasted_iota(jnp.int32, (LANES, LANES), 1)
    same_head = (vi < RWKV_HEAD) == (vj < RWKV_HEAD)

    def head_rows(x):
        xb = x.astype(bf16)
        zero = jnp.zeros((), bf16)
        return jnp.concatenate([jnp.where(first_half, xb, zero), jnp.where(first_half, zero, xb)], axis=0)

    def shifted(x_ref, prev_ref, mu_ref, g):
        x = x_ref[g]
        prev = jnp.where(row == 0, prev_ref[g], pltpu.roll(x, 1, 0))
        prev_ref[g] = x[C - 1:C, :]
        return x + (prev - x) * mu_ref[...]

    units = [(g, p) for g in range(G) for p in range(n_pairs)]
    lanes_of = lambda p: slice(p * LANES, (p + 1) * LANES)
    v_all, gate_all, rkk_all = [], [], []
    a_t, b_t, k_t, r_t, g_end, s0 = [], [], [], [], [], []
    for g in range(G):
        r = shifted(r_ref, pr_ref, mur_ref, g)
        k = shifted(k_ref, pk_ref, muk_ref, g)
        v = shifted(v_ref, pv_ref, muv_ref, g)
        lo = shifted(lo_ref, plo_ref, mulo_ref, g)
        wa_in = lo[:, :LANES]
        wa_in = jnp.where(first_half, 1.0 - 2.0 / (1.0 + jnp.exp(2.0 * wa_in)), wa_in).astype(bf16)
        wa = _mm(wa_in, wwa_ref[...])
        logdecay = -_EXP_NEG_HALF / (1.0 + jnp.exp(-(w0_ref[...] + wa[:, :RWKV_WIDTH])))
        a = 1.0 / (1.0 + jnp.exp(-(a0_ref[...] + wa[:, RWKV_WIDTH:])))
        gate_all.append(_mm((1.0 / (1.0 + jnp.exp(-lo[:, LANES:]))).astype(bf16), gup_ref[...]))
        ld_hi = logdecay.astype(bf16)
        ld_r1 = logdecay - ld_hi.astype(f32)
        ld_mid = ld_r1.astype(bf16)
        ld_lo = (ld_r1 - ld_mid.astype(f32)).astype(bf16)
        cum = _mm(tri_b, ld_hi) + _mm(tri_b, ld_mid) + _mm(tri_b, ld_lo)
        kk = k * kkw_ref[...]
        k2 = k * (1.0 + (a - 1.0) * kaw_ref[...])
        rkk_all.append(r * k2 * rk_ref[...])
        v_all.append(v)
        ecum = jnp.exp(cum)
        encum = jnp.exp(-cum)
        ecum_prev = jnp.exp(cum - logdecay)
        for p in range(n_pairs):
            sl = lanes_of(p)
            kkp = kk[:, sl]
            kkn = kkp * lax.rsqrt(jnp.maximum(_seg_sum(kkp * kkp, first_half), 1e-24))
            a_t.append(-kkn * ecum_prev[:, sl])
            b_t.append(kkn * a[:, sl] * encum[:, sl])
            k_t.append(k2[:, sl] * encum[:, sl])
            r_t.append(r[:, sl] * ecum[:, sl])
            g_end.append(ecum[C - 1:C, sl])
            s0.append(st_ref[g * n_pairs + p])

    n = len(units)
    vp = [v_all[g][:, lanes_of(p)] for g, p in units]
    v_rows = [head_rows(x) for x in vp]
    ar = [jnp.concatenate([a_t[i], r_t[i]], axis=0).astype(bf16) for i in range(n)]
    ar0 = [_nt(ar[i], s0[i].astype(bf16)) for i in range(n)]
    gram = [_nt(ar[i], jnp.concatenate([head_rows(b_t[i]), head_rows(k_t[i])], axis=0)) for i in range(n)]
    l_b = [jnp.where(tri2_strict, x[:C, :2 * C], 0.0) for x in gram]
    l_k = [jnp.where(tri2_strict, x[:C, 2 * C:], 0.0).astype(bf16) for x in gram]
    m_bk = [jnp.where(tri4_incl, x[C:, :], 0.0).astype(bf16) for x in gram]
    w = [ar0[i][:C] + _mm(l_k[i], v_rows[i]) for i in range(n)]
    inv = [eye2 + x for x in l_b]
    l_pow = l_b
    l_pow_rows = [head_rows(x) for x in l_pow]
    for _ in range(C.bit_length() - 2):
        l_pow = [_mm(x.astype(bf16), y) for x, y in zip(l_pow, l_pow_rows)]
        l_pow_rows = [head_rows(x) for x in l_pow]
        inv = [x + _mm(x.astype(bf16), y) for x, y in zip(inv, l_pow_rows)]
    u = [_mm(inv[i].astype(bf16), head_rows(w[i])) for i in range(n)]
    y_all = [ar0[i][C:] + _mm(m_bk[i], jnp.concatenate([head_rows(u[i]), v_rows[i]], axis=0))
             for i in range(n)]
    upd = [_mm(jnp.concatenate([u[i], vp[i]], axis=0).T.astype(bf16),
               (jnp.concatenate([b_t[i], k_t[i]], axis=0) * g_end[i]).astype(bf16)) for i in range(n)]
    for i, (g, p) in enumerate(units):
        sl = lanes_of(p)
        st_ref[g * n_pairs + p] = s0[i] * g_end[i] + jnp.where(same_head, upd[i], 0.0)
        y = y_all[i]
        mean = _seg_sum(y, first_half) * (1.0 / RWKV_HEAD)
        yc = y - mean
        var = _seg_sum(yc * yc, first_half) * (1.0 / RWKV_HEAD)
        yn = yc * lax.rsqrt(var + RWKV_LN_EPS) * lnw_ref[:, sl] + lnb_ref[:, sl]
        bonus = _seg_sum(rkk_all[g][:, sl], first_half) * vp[i]
        y_ref[g, :, sl] = ((yn + bonus) * gate_all[g][:, sl]).astype(y_ref.dtype)


def _rwkv(p, batch, seq, mu, w0, a0, k_k, k_a, lnx_w, lnx_b, r_k, w_wa, g_up_p):
    C = RWKV_CHUNK
    G = RWKV_BATCH_GROUP if batch % RWKV_BATCH_GROUP == 0 else 1
    W = RWKV_WIDTH
    p3 = p.reshape(batch, seq, p.shape[-1])
    row_block = lambda width, col: pl.BlockSpec((G, C, width), lambda b, c: (b, c, col // width))
    vec = lambda width: pl.BlockSpec((1, width), lambda b, c: (0, 0))
    full = lambda arr: pl.BlockSpec(arr.shape, lambda b, c: (0, 0))
    mu_r, mu_k, mu_v, mu_lo = mu
    y = pl.pallas_call(
        _rwkv_kernel,
        grid=(batch // G, seq // C),
        in_specs=[row_block(W, COL_R), row_block(W, COL_K), row_block(W, COL_V),
                  row_block(LORA_BLOCK, COL_LORA),
                  vec(W), vec(W), vec(W), vec(LORA_BLOCK),
                  vec(W), vec(W), vec(W), vec(W), vec(W), vec(W), vec(W),
                  full(w_wa), full(g_up_p)],
        out_specs=pl.BlockSpec((G, C, W), lambda b, c: (b, c, 0)),
        out_shape=jax.ShapeDtypeStruct((batch, seq, W), bf16),
        scratch_shapes=[pltpu.VMEM((G * W // LANES, LANES, LANES), f32),
                        pltpu.VMEM((G, 1, W), f32), pltpu.VMEM((G, 1, W), f32), pltpu.VMEM((G, 1, W), f32),
                        pltpu.VMEM((G, 1, LORA_BLOCK), f32)],
        compiler_params=_params("parallel", "arbitrary"),
        name="rwkv7_chunked",
    )(p3, p3, p3, p3, mu_r, mu_k, mu_v, mu_lo, w0, a0, k_k, k_a, lnx_w, lnx_b, r_k, w_wa, g_up_p)
    return y.reshape(batch * seq, W)


def _rope(x, cos, sin_lo, sin_hi, half):
    n = x.shape[-1]
    return x * cos + pltpu.roll(x, n - half, 1) * sin_lo + pltpu.roll(x, half, 1) * sin_hi


def _dsa_prep_kernel(pos_ref, freq_ref, q_ref, qi_ref, ka_ref, va_ref, kiwi_ref,
                     qo_ref, qio_ref, ko_ref, vo_ref, ki2_ref, wi_ref):
    pos = pos_ref[...]
    lane = lax.broadcasted_iota(jnp.int32, (1, LANES), 1)
    ang = pos * freq_ref[...]
    cos_t, sin_t = jnp.cos(ang), jnp.sin(ang)

    def tables(cos, sin, head, half):
        in_head = lane % head
        sin_lo = jnp.where(in_head < half, -sin, 0.0)
        sin_hi = jnp.where((in_head >= half) & (in_head < 2 * half), sin, 0.0)
        return cos, sin_lo, sin_hi

    att_lanes = lane < ATT_HEAD // 4
    upper = lane >= IDX_HEAD
    ca, sla, sha = tables(jnp.where(att_lanes, cos_t, 1.0), jnp.where(att_lanes, sin_t, 0.0),
                          ATT_HEAD, ATT_HEAD // 8)
    ci, sli, shi = tables(jnp.where(upper, cos_t, pltpu.roll(cos_t, IDX_HEAD, 1)),
                          jnp.where(upper, sin_t, pltpu.roll(sin_t, IDX_HEAD, 1)),
                          IDX_HEAD, IDX_HEAD // 8)
    for h in range(ATT_HEADS):
        sl = slice(h * LANES, (h + 1) * LANES)
        qo_ref[:, sl] = _rope(q_ref[:, sl], ca, sla, sha, ATT_HEAD // 8).astype(bf16)
        qio_ref[:, sl] = _rope(qi_ref[:, sl], ci, sli, shi, IDX_HEAD // 8).astype(bf16)
    for h in range(ATT_KV_HEADS):
        sl = slice(h * LANES, (h + 1) * LANES)
        ko_ref[:, sl] = _rope(ka_ref[:, sl], ca, sla, sha, ATT_HEAD // 8).astype(bf16)
    vo_ref[...] = va_ref[...].astype(bf16)
    kiwi = kiwi_ref[...]
    ki_only = jnp.where(lane < IDX_HEAD, kiwi, 0.0)
    ki = _rope(ki_only, ci, sli, shi, IDX_HEAD // 8)
    ki2_ref[...] = (ki + pltpu.roll(ki, IDX_HEAD, 1)).astype(bf16)
    wi = pltpu.roll(kiwi, LANES - IDX_HEAD, 1)
    wi_ref[...] = jnp.where(lane < IDX_HEADS, wi, 0.0) * (IDX_HEADS ** -0.5 * IDX_HEAD ** -0.5)


def _dsa_prep(p, pos, freq, tm):
    t = p.shape[0]
    blk = lambda width, col: pl.BlockSpec((tm, width), lambda i: (i, col // width))
    out = lambda width: pl.BlockSpec((tm, width), lambda i: (i, 0))
    vec = pl.BlockSpec((1, LANES), lambda i: (0, 0))
    kvw = ATT_KV_HEADS * ATT_HEAD
    return pl.pallas_call(
        _dsa_prep_kernel,
        grid=(t // tm,),
        in_specs=[pl.BlockSpec((tm, 1), lambda i: (i, 0)), vec,
                  blk(ATT_WIDTH, COL_Q), blk(IDX_HEADS * IDX_HEAD, COL_QI),
                  blk(kvw, COL_KATT), blk(kvw, COL_VATT), blk(LANES, COL_KIWI)],
        out_specs=[out(ATT_WIDTH), out(IDX_HEADS * IDX_HEAD), out(kvw), out(kvw), out(LANES), out(LANES)],
        out_shape=[jax.ShapeDtypeStruct((t, ATT_WIDTH), bf16),
                   jax.ShapeDtypeStruct((t, IDX_HEADS * IDX_HEAD), bf16),
                   jax.ShapeDtypeStruct((t, kvw), bf16),
                   jax.ShapeDtypeStruct((t, kvw), bf16),
                   jax.ShapeDtypeStruct((t, LANES), bf16),
                   jax.ShapeDtypeStruct((t, LANES), f32)],
        compiler_params=_params("parallel"),
        name="dsa_prep",
    )(pos, freq, p, p, p, p, p)


def _dsa_kernel(q_ref, qi_ref, wi_ref, k_ref, v_ref, ki2_ref, o_ref, score_ref, key_ref, logit_ref,
                *, n_sel, bucket_len):
    tq = q_ref.shape[0]
    seq = k_ref.shape[0]
    bucket = ((pl.program_id(1) + 1) * tq - 1) // bucket_len
    for j in range(seq // bucket_len):
        pl.when(bucket == j)(functools.partial(
            _dsa_body, q_ref, qi_ref, wi_ref, k_ref, v_ref, ki2_ref, o_ref, score_ref, key_ref, logit_ref,
            n_sel=n_sel, s_len=(j + 1) * bucket_len))


def _dsa_body(q_ref, qi_ref, wi_ref, k_ref, v_ref, ki2_ref, o_ref, score_ref, key_ref, logit_ref,
              *, n_sel, s_len):
    tq = q_ref.shape[0]
    q0 = pl.program_id(1) * tq
    lane = lax.broadcasted_iota(jnp.int32, (1, LANES), 1)
    first_half = lane < IDX_HEAD
    ki2 = ki2_ref[0:s_len, :]
    wi = wi_ref[...]

    score = jnp.zeros((tq, s_len), f32)
    for h in range(IDX_HEADS):
        pair = h // 2
        m = first_half if h % 2 == 0 else jnp.logical_not(first_half)
        qm = jnp.where(m, qi_ref[:, pair * LANES:(pair + 1) * LANES], jnp.zeros((), bf16))
        rel = jnp.maximum(_nt(qm, ki2), 0.0)
        w_h = jnp.sum(jnp.where(lane == h, wi, 0.0), axis=-1, keepdims=True)
        score = score + rel * w_h

    group = ATT_HEADS // ATT_KV_HEADS
    for kv in range(ATT_KV_HEADS):
        qg = jnp.concatenate([q_ref[:, (kv * group + g) * LANES:(kv * group + g + 1) * LANES]
                              for g in range(group)], axis=0)
        logit_ref[kv, :, 0:s_len] = _nt(qg, k_ref[0:s_len, kv * LANES:(kv + 1) * LANES]) * (ATT_HEAD ** -0.5)

    q_pos = q0 + lax.broadcasted_iota(jnp.int32, (tq, 1), 0)
    k_pos = lax.broadcasted_iota(jnp.int32, (1, s_len), 1)
    causal = k_pos <= q_pos
    score = jnp.where(causal, score, -1e30) + 0.0
    score_ref[:, 0:s_len] = score

    n8 = s_len // SUBLANES
    bits = pltpu.bitcast(score.T, jnp.int32)
    key_t = jnp.where(bits >= 0, bits, bits ^ jnp.int32(0x7FFFFFFF))
    key_ref[0:n8] = key_t.reshape(n8, SUBLANES, tq)
    int_min = jnp.int32(-2 ** 31)

    def search(i, t_u):
        cand = t_u | lax.shift_right_logical(int_min, i)
        ge = key_ref[0:n8] >= (cand ^ int_min)[None]
        ones = jnp.where(ge, 1.0, 0.0).reshape(n8 // COUNT_CHAINS, COUNT_CHAINS, SUBLANES, tq)
        cnt = jnp.sum(jnp.sum(ones, axis=0), axis=0)
        for shift in (4, 2, 1):
            cnt = cnt + pltpu.roll(cnt, shift, 0)
        return jnp.where(cnt >= n_sel, cand, t_u)

    t_u = lax.fori_loop(0, 32, search, jnp.zeros((SUBLANES, tq), jnp.int32), unroll=4)
    t_key = t_u[0:1, :] ^ int_min
    thr_row = pltpu.bitcast(jnp.where(t_key >= 0, t_key, t_key ^ jnp.int32(0x7FFFFFFF)), f32)
    diag = lax.broadcasted_iota(jnp.int32, (tq, tq), 0) == lax.broadcasted_iota(jnp.int32, (tq, tq), 1)
    thr_col = jnp.sum(jnp.where(diag, thr_row, 0.0), axis=-1, keepdims=True)
    selected = (score_ref[:, 0:s_len] >= thr_col) & causal

    sel_g = jnp.concatenate([selected] * group, axis=0)
    probs, denoms = [], []
    for kv in range(ATT_KV_HEADS):
        s = jnp.where(sel_g, logit_ref[kv, :, 0:s_len], -jnp.inf)
        e = jnp.exp(s - jnp.max(s, axis=-1, keepdims=True))
        denoms.append(jnp.sum(e, axis=-1, keepdims=True))
        probs.append(e.astype(bf16))
    for kv in range(ATT_KV_HEADS):
        o = _mm(probs[kv], v_ref[0:s_len, kv * LANES:(kv + 1) * LANES]) / denoms[kv]
        for g in range(group):
            hq = kv * group + g
            o_ref[:, hq * LANES:(hq + 1) * LANES] = o[g * tq:(g + 1) * tq].astype(o_ref.dtype)


def _dsa(q, qi, wi, k, v, ki2, batch, seq, tq):
    nq = seq // tq
    n_sel = min(TOPK_MAX, seq // 4)
    qblk = lambda width: pl.BlockSpec((tq, width), lambda b, i: (b * nq + i, 0))
    kblk = lambda width: pl.BlockSpec((seq, width), lambda b, i: (b, 0))
    kvw = ATT_KV_HEADS * ATT_HEAD
    bucket_len = max(tq, seq // DSA_BUCKETS)
    return pl.pallas_call(
        functools.partial(_dsa_kernel, n_sel=n_sel, bucket_len=bucket_len),
        grid=(batch, nq),
        in_specs=[qblk(ATT_WIDTH), qblk(IDX_HEADS * IDX_HEAD), qblk(LANES),
                  kblk(kvw), kblk(kvw), kblk(LANES)],
        out_specs=qblk(ATT_WIDTH),
        out_shape=jax.ShapeDtypeStruct((batch * seq, ATT_WIDTH), bf16),
        scratch_shapes=[pltpu.VMEM((tq, seq), f32),
                        pltpu.VMEM((seq // SUBLANES, SUBLANES, tq), jnp.int32),
                        pltpu.VMEM((ATT_KV_HEADS, (ATT_HEADS // ATT_KV_HEADS) * tq, seq), f32)],
        compiler_params=_params("parallel", "arbitrary"),
        name="dsa_attention",
    )(q, qi, wi, k, v, ki2)


def _mix_out_kernel(x_ref, ya_ref, yb_ref, wa_ref, wb_ref, o_ref):
    o_ref[...] = x_ref[...] + _mm(ya_ref[...], wa_ref[...]) + _mm(yb_ref[...], wb_ref[...])


def _mix_out(x, ya, yb, w, tm, tn):
    t, d = x.shape
    ka = ya.shape[1]
    kb = yb.shape[1]
    return pl.pallas_call(
        _mix_out_kernel,
        grid=(t // tm, d // tn),
        in_specs=[pl.BlockSpec((tm, tn), lambda i, j: (i, j)),
                  pl.BlockSpec((tm, ka), lambda i, j: (i, 0)),
                  pl.BlockSpec((tm, kb), lambda i, j: (i, 0)),
                  pl.BlockSpec((ka, tn), lambda i, j: (0, j)),
                  pl.BlockSpec((kb, tn), lambda i, j: (ka // kb, j))],
        out_specs=pl.BlockSpec((tm, tn), lambda i, j: (i, j)),
        out_shape=jax.ShapeDtypeStruct((t, d), f32),
        compiler_params=_params("parallel", "arbitrary"),
        name="mix_out",
    )(x, ya, yb, w, w)


def _cross_kernel(h_ref, g_ref, kv_ref, wq_ref, wo_ref, o_ref):
    h = h_ref[...]
    hn = _rms(h, g_ref[...]).astype(bf16)
    q = _mm(hn, wq_ref[...]).astype(bf16)
    width = CROSS_HEADS * CROSS_HEAD
    cols = lambda hd, base=0: slice(base + hd * CROSS_HEAD, base + (hd + 1) * CROSS_HEAD)
    logits = [_nt(q[:, cols(hd)], kv_ref[:, cols(hd)]) * (CROSS_HEAD ** -0.5) for hd in range(CROSS_HEADS)]
    exps = [jnp.exp(s - jnp.max(s, axis=-1, keepdims=True)) for s in logits]
    outs = [(_mm(e.astype(bf16), kv_ref[:, cols(hd, width)]) / jnp.sum(e, axis=-1, keepdims=True)).astype(bf16)
            for hd, e in enumerate(exps)]
    o = jnp.concatenate(outs, axis=-1)
    o_ref[...] = h + _mm(o, wo_ref[...])


def _cross(h, g, kv, wq, wo, batch, seq, n_mem, tm):
    d = h.shape[1]
    ns = seq // tm
    width = CROSS_HEADS * CROSS_HEAD
    return pl.pallas_call(
        _cross_kernel,
        grid=(batch, ns),
        in_specs=[pl.BlockSpec((tm, d), lambda b, i: (b * ns + i, 0)),
                  pl.BlockSpec((1, d), lambda b, i: (0, 0)),
                  pl.BlockSpec((n_mem, 2 * width), lambda b, i: (b, 0)),
                  pl.BlockSpec((d, width), lambda b, i: (0, 0)),
                  pl.BlockSpec((width, d), lambda b, i: (0, 0))],
        out_specs=pl.BlockSpec((tm, d), lambda b, i: (b * ns + i, 0)),
        out_shape=jax.ShapeDtypeStruct(h.shape, f32),
        compiler_params=_params("parallel", "arbitrary"),
        name="cross_attention",
    )(h, g, kv, wq, wo)


def _mlp_kernel(h_ref, g_ref, wu_ref, wd_ref, gf_ref, o_ref, hn_ref, *, final_norm):
    j = pl.program_id(1)

    @pl.when(j == 0)
    def _():
        h = h_ref[...]
        hn_ref[...] = _rms(h, g_ref[...]).astype(bf16)
        o_ref[...] = h

    u = jnp.maximum(_mm(hn_ref[...], wu_ref[...]), 0.0)
    o_ref[...] += _mm((u * u).astype(bf16), wd_ref[...])

    if final_norm:
        @pl.when(j == pl.num_programs(1) - 1)
        def _():
            o_ref[...] = _rms(o_ref[...], gf_ref[...])


def _mlp(h, g, wu, wd, gf, final_norm, tm, tf):
    t, d = h.shape
    dff = wu.shape[1]
    return pl.pallas_call(
        functools.partial(_mlp_kernel, final_norm=final_norm),
        grid=(t // tm, dff // tf),
        in_specs=[pl.BlockSpec((tm, d), lambda i, j: (i, 0)),
                  pl.BlockSpec((1, d), lambda i, j: (0, 0)),
                  pl.BlockSpec((d, tf), lambda i, j: (0, j)),
                  pl.BlockSpec((tf, d), lambda i, j: (j, 0)),
                  pl.BlockSpec((1, d), lambda i, j: (0, 0))],
        out_specs=pl.BlockSpec((tm, d), lambda i, j: (i, 0)),
        out_shape=jax.ShapeDtypeStruct((t, d), f32),
        scratch_shapes=[pltpu.VMEM((tm, d), bf16)],
        compiler_params=_params("parallel", "arbitrary"),
        name="mlp_final_norm",
    )(h, g, wu, wd, gf)


def _pack_in_proj(w_in, mu):
    d = w_in.shape[0]
    o = 0
    seg = {}
    for name, width in (("r", 1024), ("k", 1024), ("v", 1024), ("xw", 64), ("xa", 64), ("xg", 160),
                        ("q", 1024), ("katt", 256), ("vatt", 256), ("qi", 1024), ("ki", 64), ("wi", 16)):
        seg[name] = (o, o + width)
        o += width
    w_in = w_in.astype(bf16)
    col = lambda n: w_in[:, seg[n][0]:seg[n][1]]
    zeros = lambda n: jnp.zeros((d, n), w_in.dtype)
    w = jnp.concatenate([col("r"), col("k"), col("v"), col("q"), col("qi"), col("katt"), col("vatt"),
                         col("ki"), col("wi"), zeros(LANES - 80),
                         col("xw"), col("xa"), col("xg"), zeros(LORA_BLOCK - 288)], axis=1)
    mu_lo = jnp.concatenate([mu[3072:3360], jnp.zeros((LORA_BLOCK - 288,), mu.dtype)])
    mus = (mu[0:1024][None], mu[1024:2048][None], mu[2048:3072][None], mu_lo[None])
    return w, mus


def _rope_freq():
    def inv_freq(head):
        half = head // 8
        return ROPE_THETA ** (-jnp.arange(half, dtype=f32) / half)
    lane = jnp.arange(LANES)
    att = inv_freq(ATT_HEAD)[lane % (ATT_HEAD // 8)]
    idx = inv_freq(IDX_HEAD)[lane % (IDX_HEAD // 8)]
    in_idx = (lane >= IDX_HEAD) & (lane < IDX_HEAD + IDX_HEAD // 4)
    return jnp.where(lane < ATT_HEAD // 4, att, jnp.where(in_idx, idx, 0.0)).astype(f32)[None]


def kernel(x, mem, positions, norm_mix, w_in, rwkv_mu, w_decay0, w_decay_up, a0, a_up, g_up, k_k, k_a, r_k,
           lnx_w, lnx_b, w_mix_out, norm_cross, norm_mem, w_q_cross, w_kv_cross, w_o_cross, norm_mlp,
           w_up, w_down, norm_final):
    batch, seq, d = x.shape
    n_mem = mem.shape[1]
    t = batch * seq
    depth = w_in.shape[0]
    tm = min(512, t)
    h = x.reshape(t, d)
    pos = positions.reshape(t, 1).astype(f32)
    mem2 = mem.reshape(batch * n_mem, d)
    row = lambda vct: vct.reshape(1, -1)

    for l in range(depth):
        w_in_p, mus = _pack_in_proj(w_in[l], rwkv_mu[l])
        zpad = jnp.zeros((DECAY_LORA, RWKV_WIDTH), f32)
        w_wa = jnp.concatenate([jnp.concatenate([w_decay_up[l], zpad], axis=1),
                                jnp.concatenate([zpad, a_up[l]], axis=1)], axis=0).astype(bf16)
        g_up_p = jnp.concatenate([g_up[l], jnp.zeros((LORA_BLOCK - LANES - GATE_LORA, RWKV_WIDTH), f32)],
                                 axis=0).astype(bf16)

        p = _norm_matmul(h, row(norm_mix[l]), w_in_p, f32, min(1024, t), 1024)
        y_rwkv = _rwkv(p, batch, seq, mus, row(w_decay0[l]), row(a0[l]), row(k_k[l]), row(k_a[l]),
                       row(lnx_w[l]), row(lnx_b[l]), row(r_k[l]), w_wa, g_up_p)
        q, qi, k_att, v_att, ki2, wi = _dsa_prep(p, pos, _rope_freq(), tm)
        y_att = _dsa(q, qi, wi, k_att, v_att, ki2, batch, seq, min(DSA_QUERY_BLOCK, seq))
        h = _mix_out(h, y_rwkv, y_att, w_mix_out[l].astype(bf16), tm, d)

        kv = _norm_matmul(mem2, row(norm_mem[l]), w_kv_cross[l].astype(bf16), bf16, min(512, batch * n_mem), 1024)
        h = _cross(h, row(norm_cross[l]), kv, w_q_cross[l].astype(bf16), w_o_cross[l].astype(bf16),
                   batch, seq, n_mem, tm)
        h = _mlp(h, row(norm_mlp[l]), w_up[l].astype(bf16), w_down[l].astype(bf16), row(norm_final),
                 l == depth - 1, tm, 1024)
    return h.reshape(batch, seq, d)
```

```python
import functools

import jax
import jax.numpy as jnp
from jax import lax
from jax.experimental import pallas as pl
from jax.experimental.pallas import tpu as pltpu

f32 = jnp.float32
bf16 = jnp.bfloat16

D_MODEL = 2048
RWKV_WIDTH = 1024
RWKV_HEAD = 64
DECAY_LORA = 64
AAA_LORA = 64
GATE_LORA = 160
RWKV_LN_EPS = 64e-5
ATT_WIDTH = 1024
ATT_HEAD = 128
ATT_HEADS = 8
ATT_KV_HEADS = 2
IDX_HEADS = 16
IDX_HEAD = 64
TOPK_MAX = 256
ROPE_THETA = 500000.0
CROSS_HEADS = 4
CROSS_HEAD = 128
D_FF = 4 * D_MODEL
NORM_EPS = 1e-5

LANES = 128
SUBLANES = 8
VMEM_LIMIT_BYTES = 56 * 1024 * 1024

COL_R, COL_K, COL_V, COL_LORA = 0, 1024, 2048, 3072
LORA_BLOCK = 384
RWKV_COLS, ATT_COLS = 3360, 2640
COL_Q = COL_LORA + LORA_BLOCK
COL_KATT = COL_Q + ATT_WIDTH
COL_VATT = COL_KATT + ATT_KV_HEADS * ATT_HEAD
COL_QI = COL_VATT + ATT_KV_HEADS * ATT_HEAD
COL_KIWI = COL_QI + IDX_HEADS * IDX_HEAD
IN_COLS_PACKED = COL_KIWI + LANES
RWKV_CHUNK = 64
_EXP_NEG_HALF = 0.6065306597126334
RWKV_BATCH_GROUP = 2
DSA_QUERY_BLOCK = 128
COUNT_CHAINS = 8
DSA_BUCKETS = 8


def _nt(a, b):
    return lax.dot_general(a, b, (((1,), (1,)), ((), ())), preferred_element_type=f32)


def _mm(a, b):
    return jnp.dot(a, b, preferred_element_type=f32)


def _rms(x, g):
    ms = jnp.mean(x * x, axis=-1, keepdims=True)
    return x * lax.rsqrt(ms + NORM_EPS) * g


def _params(*sem):
    return pltpu.CompilerParams(dimension_semantics=sem, vmem_limit_bytes=VMEM_LIMIT_BYTES)


def _norm_matmul_kernel(x_ref, g_ref, w_ref, o_ref, xn_ref):
    @pl.when(pl.program_id(1) == 0)
    def _():
        xn_ref[...] = _rms(x_ref[...], g_ref[...]).astype(bf16)

    o_ref[...] = _mm(xn_ref[...], w_ref[...]).astype(o_ref.dtype)


def _norm_matmul(x, g, w, out_dtype, tm, tn):
    t, d = x.shape
    n = w.shape[1]
    return pl.pallas_call(
        _norm_matmul_kernel,
        grid=(t // tm, n // tn),
        in_specs=[
            pl.BlockSpec((tm, d), lambda i, j: (i, 0)),
            pl.BlockSpec((1, d), lambda i, j: (0, 0)),
            pl.BlockSpec((d, tn), lambda i, j: (0, j)),
        ],
        out_specs=pl.BlockSpec((tm, tn), lambda i, j: (i, j)),
        out_shape=jax.ShapeDtypeStruct((t, n), out_dtype),
        scratch_shapes=[pltpu.VMEM((tm, d), bf16)],
        compiler_params=_params("parallel", "arbitrary"),
        name="norm_matmul",
    )(x, g, w)


def _seg_sum(x, first_half):
    s_a = jnp.sum(jnp.where(first_half, x, 0.0), axis=-1, keepdims=True)
    s_b = jnp.sum(jnp.where(first_half, 0.0, x), axis=-1, keepdims=True)
    return jnp.where(first_half, s_a, s_b)


def _rwkv_kernel(r_ref, k_ref, v_ref, lo_ref, mur_ref, muk_ref, muv_ref, mulo_ref,
                 w0_ref, a0_ref, kkw_ref, kaw_ref, lnw_ref, lnb_ref, rk_ref, wwa_ref, gup_ref,
                 y_ref, st_ref, pr_ref, pk_ref, pv_ref, plo_ref):
    G, C = r_ref.shape[0], r_ref.shape[1]
    n_pairs = RWKV_WIDTH // LANES

    @pl.when(pl.program_id(1) == 0)
    def _():
        st_ref[...] = jnp.zeros_like(st_ref)
        pr_ref[...] = jnp.zeros_like(pr_ref)
        pk_ref[...] = jnp.zeros_like(pk_ref)
        pv_ref[...] = jnp.zeros_like(pv_ref)
        plo_ref[...] = jnp.zeros_like(plo_ref)

    row = lax.broadcasted_iota(jnp.int32, (C, 1), 0)
    lane = lax.broadcasted_iota(jnp.int32, (1, LANES), 1)
    first_half = lane < RWKV_HEAD
    ti = lax.broadcasted_iota(jnp.int32, (C, C), 0)
    tj = lax.broadcasted_iota(jnp.int32, (C, C), 1)
    tri_b = jnp.where(ti >= tj, 1.0, 0.0).astype(bf16)
    t2 = lax.broadcasted_iota(jnp.int32, (C, 2 * C), 0)
    j2 = lax.broadcasted_iota(jnp.int32, (C, 2 * C), 1) % C
    tri2_strict = t2 > j2
    eye2 = jnp.where(t2 == j2, 1.0, 0.0)
    t4 = lax.broadcasted_iota(jnp.int32, (C, 4 * C), 0)
    j4 = lax.broadcasted_iota(jnp.int32, (C, 4 * C), 1) % C
    tri4_incl = t4 >= j4
    vi = lax.broadcasted_iota(jnp.int32, (LANES, LANES), 0)
    vj = lax.broadcasted_iota(jnp.int32, (LANES, LANES), 1)
    same_head = (vi < RWKV_HEAD) == (vj < RWKV_HEAD)

    def head_rows(x):
        xb = x.astype(bf16)
        zero = jnp.zeros((), bf16)
        return jnp.concatenate([jnp.where(first_half, xb, zero), jnp.where(first_half, zero, xb)], axis=0)

    def shifted(x_ref, prev_ref, mu_ref, g):
        x = x_ref[g]
        prev = jnp.where(row == 0, prev_ref[g], pltpu.roll(x, 1, 0))
        prev_ref[g] = x[C - 1:C, :]
        return x + (prev - x) * mu_ref[...]

    units = [(g, p) for g in range(G) for p in range(n_pairs)]
    lanes_of = lambda p: slice(p * LANES, (p + 1) * LANES)
    v_all, gate_all, rkk_all = [], [], []
    a_t, b_t, k_t, r_t, g_end, s0 = [], [], [], [], [], []
    for g in range(G):
        r = shifted(r_ref, pr_ref, mur_ref, g)
        k = shifted(k_ref, pk_ref, muk_ref, g)
        v = shifted(v_ref, pv_ref, muv_ref, g)
        lo = shifted(lo_ref, plo_ref, mulo_ref, g)
        wa_in = lo[:, :LANES]
        wa_in = jnp.where(first_half, 1.0 - 2.0 / (1.0 + jnp.exp(2.0 * wa_in)), wa_in).astype(bf16)
        wa = _mm(wa_in, wwa_ref[...])
        logdecay = -_EXP_NEG_HALF / (1.0 + jnp.exp(-(w0_ref[...] + wa[:, :RWKV_WIDTH])))
        a = 1.0 / (1.0 + jnp.exp(-(a0_ref[...] + wa[:, RWKV_WIDTH:])))
        gate_all.append(_mm((1.0 / (1.0 + jnp.exp(-lo[:, LANES:]))).astype(bf16), gup_ref[...]))
        ld_hi = logdecay.astype(bf16)
        ld_r1 = logdecay - ld_hi.astype(f32)
        ld_mid = ld_r1.astype(bf16)
        ld_lo = (ld_r1 - ld_mid.astype(f32)).astype(bf16)
        cum = _mm(tri_b, ld_hi) + _mm(tri_b, ld_mid) + _mm(tri_b, ld_lo)
        kk = k * kkw_ref[...]
        k2 = k * (1.0 + (a - 1.0) * kaw_ref[...])
        rkk_all.append(r * k2 * rk_ref[...])
        v_all.append(v)
        ecum = jnp.exp(cum)
        encum = jnp.exp(-cum)
        ecum_prev = jnp.exp(cum - logdecay)
        for p in range(n_pairs):
            sl = lanes_of(p)
            kkp = kk[:, sl]
            kkn = kkp * lax.rsqrt(jnp.maximum(_seg_sum(kkp * kkp, first_half), 1e-24))
            a_t.append(-kkn * ecum_prev[:, sl])
            b_t.append(kkn * a[:, sl] * encum[:, sl])
            k_t.append(k2[:, sl] * encum[:, sl])
            r_t.append(r[:, sl] * ecum[:, sl])
            g_end.append(ecum[C - 1:C, sl])
            s0.append(st_ref[g * n_pairs + p])

    n = len(units)
    vp = [v_all[g][:, lanes_of(p)] for g, p in units]
    v_rows = [head_rows(x) for x in vp]
    ar = [jnp.concatenate([a_t[i], r_t[i]], axis=0).astype(bf16) for i in range(n)]
    ar_all = [_nt(ar[i], jnp.concatenate([s0[i].astype(bf16), head_rows(b_t[i]), head_rows(k_t[i])], axis=0))
              for i in range(n)]
    ar0 = [x[:, :LANES] for x in ar_all]
    gram = [x[:, LANES:] for x in ar_all]
    l_b = [jnp.where(tri2_strict, x[:C, :2 * C], 0.0) for x in gram]
    l_k = [jnp.where(tri2_strict, x[:C, 2 * C:], 0.0).astype(bf16) for x in gram]
    m_bk = [jnp.where(tri4_incl, x[C:, :], 0.0).astype(bf16) for x in gram]
    w = [ar0[i][:C] + _mm(l_k[i], v_rows[i]) for i in range(n)]
    inv = [eye2 + x for x in l_b]
    l_pow_rows = [head_rows(x) for x in l_b]
    l_pow = [_mm(x.astype(bf16), y) for x, y in zip(l_b, l_pow_rows)]
    n_steps = C.bit_length() - 2
    for step in range(n_steps):
        l_pow_rows = [head_rows(x) for x in l_pow]
        if step < n_steps - 1:
            both = [_mm(jnp.concatenate([x, y], axis=0).astype(bf16), z)
                    for x, y, z in zip(inv, l_pow, l_pow_rows)]
            inv = [x + y[:C] for x, y in zip(inv, both)]
            l_pow = [y[C:] for y in both]
        else:
            inv = [x + _mm(x.astype(bf16), z) for x, z in zip(inv, l_pow_rows)]
    u = [_mm(inv[i].astype(bf16), head_rows(w[i])) for i in range(n)]
    y_all = [ar0[i][C:] + _mm(m_bk[i], jnp.concatenate([head_rows(u[i]), v_rows[i]], axis=0))
             for i in range(n)]
    upd = [_mm(jnp.concatenate([u[i], vp[i]], axis=0).T.astype(bf16),
               (jnp.concatenate([b_t[i], k_t[i]], axis=0) * g_end[i]).astype(bf16)) for i in range(n)]
    for i, (g, p) in enumerate(units):
        sl = lanes_of(p)
        st_ref[g * n_pairs + p] = s0[i] * g_end[i] + jnp.where(same_head, upd[i], 0.0)
        y = y_all[i]
        mean = _seg_sum(y, first_half) * (1.0 / RWKV_HEAD)
        yc = y - mean
        var = _seg_sum(yc * yc, first_half) * (1.0 / RWKV_HEAD)
        yn = yc * lax.rsqrt(var + RWKV_LN_EPS) * lnw_ref[:, sl] + lnb_ref[:, sl]
        bonus = _seg_sum(rkk_all[g][:, sl], first_half) * vp[i]
        y_ref[g, :, sl] = ((yn + bonus) * gate_all[g][:, sl]).astype(y_ref.dtype)


def _rwkv(p, batch, seq, mu, w0, a0, k_k, k_a, lnx_w, lnx_b, r_k, w_wa, g_up_p):
    C = RWKV_CHUNK
    G = RWKV_BATCH_GROUP if batch % RWKV_BATCH_GROUP == 0 else 1
    W = RWKV_WIDTH
    p3 = p.reshape(batch, seq, p.shape[-1])
    row_block = lambda width, col: pl.BlockSpec((G, C, width), lambda b, c: (b, c, col // width))
    vec = lambda width: pl.BlockSpec((1, width), lambda b, c: (0, 0))
    full = lambda arr: pl.BlockSpec(arr.shape, lambda b, c: (0, 0))
    mu_r, mu_k, mu_v, mu_lo = mu
    y = pl.pallas_call(
        _rwkv_kernel,
        grid=(batch // G, seq // C),
        in_specs=[row_block(W, COL_R), row_block(W, COL_K), row_block(W, COL_V),
                  row_block(LORA_BLOCK, COL_LORA),
                  vec(W), vec(W), vec(W), vec(LORA_BLOCK),
                  vec(W), vec(W), vec(W), vec(W), vec(W), vec(W), vec(W),
                  full(w_wa), full(g_up_p)],
        out_specs=pl.BlockSpec((G, C, W), lambda b, c: (b, c, 0)),
        out_shape=jax.ShapeDtypeStruct((batch, seq, W), bf16),
        scratch_shapes=[pltpu.VMEM((G * W // LANES, LANES, LANES), f32),
                        pltpu.VMEM((G, 1, W), f32), pltpu.VMEM((G, 1, W), f32), pltpu.VMEM((G, 1, W), f32),
                        pltpu.VMEM((G, 1, LORA_BLOCK), f32)],
        compiler_params=_params("parallel", "arbitrary"),
        name="rwkv7_chunked",
    )(p3, p3, p3, p3, mu_r, mu_k, mu_v, mu_lo, w0, a0, k_k, k_a, lnx_w, lnx_b, r_k, w_wa, g_up_p)
    return y.reshape(batch * seq, W)


def _rope(x, cos, sin_lo, sin_hi, half):
    n = x.shape[-1]
    return x * cos + pltpu.roll(x, n - half, 1) * sin_lo + pltpu.roll(x, half, 1) * sin_hi


def _dsa_prep_kernel(pos_ref, freq_ref, *refs):
    n_qi = IDX_HEADS * IDX_HEAD // LANES
    q_refs, refs = refs[:ATT_HEADS], refs[ATT_HEADS:]
    qi_refs, refs = refs[:n_qi], refs[n_qi:]
    ka_refs, refs = refs[:ATT_KV_HEADS], refs[ATT_KV_HEADS:]
    va_refs, refs = refs[:ATT_KV_HEADS], refs[ATT_KV_HEADS:]
    kiwi_ref, qo_ref, qio_ref, ko_ref, vo_ref, ki2_ref, wi_ref = refs
    pos = pos_ref[...]
    lane = lax.broadcasted_iota(jnp.int32, (1, LANES), 1)
    ang = pos * freq_ref[...]
    cos_t, sin_t = jnp.cos(ang), jnp.sin(ang)

    def tables(cos, sin, head, half):
        in_head = lane % head
        sin_lo = jnp.where(in_head < half, -sin, 0.0)
        sin_hi = jnp.where((in_head >= half) & (in_head < 2 * half), sin, 0.0)
        return cos, sin_lo, sin_hi

    att_lanes = lane < ATT_HEAD // 4
    upper = lane >= IDX_HEAD
    ca, sla, sha = tables(jnp.where(att_lanes, cos_t, 1.0), jnp.where(att_lanes, sin_t, 0.0),
                          ATT_HEAD, ATT_HEAD // 8)
    ci, sli, shi = tables(jnp.where(upper, cos_t, pltpu.roll(cos_t, IDX_HEAD, 1)),
                          jnp.where(upper, sin_t, pltpu.roll(sin_t, IDX_HEAD, 1)),
                          IDX_HEAD, IDX_HEAD // 8)
    lanes_of = lambda h: slice(h * LANES, (h + 1) * LANES)
    for h in range(ATT_HEADS):
        qo_ref[:, lanes_of(h)] = _rope(q_refs[h][...], ca, sla, sha, ATT_HEAD // 8).astype(bf16)
    for h in range(n_qi):
        qio_ref[:, lanes_of(h)] = _rope(qi_refs[h][...], ci, sli, shi, IDX_HEAD // 8).astype(bf16)
    for h in range(ATT_KV_HEADS):
        ko_ref[:, lanes_of(h)] = _rope(ka_refs[h][...], ca, sla, sha, ATT_HEAD // 8).astype(bf16)
        vo_ref[:, lanes_of(h)] = va_refs[h][...].astype(bf16)
    kiwi = kiwi_ref[...]
    ki_only = jnp.where(lane < IDX_HEAD, kiwi, 0.0)
    ki = _rope(ki_only, ci, sli, shi, IDX_HEAD // 8)
    ki2_ref[...] = (ki + pltpu.roll(ki, IDX_HEAD, 1)).astype(bf16)
    wi = pltpu.roll(kiwi, LANES - IDX_HEAD, 1)
    wi_ref[...] = jnp.where(lane < IDX_HEADS, wi, 0.0) * (IDX_HEADS ** -0.5 * IDX_HEAD ** -0.5)


def _dsa_prep(p, pos, freq, tm):
    t = p.shape[0]
    out = lambda width: pl.BlockSpec((tm, width), lambda i: (i, 0))
    vec = pl.BlockSpec((1, LANES), lambda i: (0, 0))
    kvw = ATT_KV_HEADS * ATT_HEAD
    head_cols = ([COL_Q + h * LANES for h in range(ATT_HEADS)]
                 + [COL_QI + h * LANES for h in range(IDX_HEADS * IDX_HEAD // LANES)]
                 + [COL_KATT + h * LANES for h in range(ATT_KV_HEADS)]
                 + [COL_VATT + h * LANES for h in range(ATT_KV_HEADS)]
                 + [COL_KIWI])
    head_blk = lambda col: pl.BlockSpec((tm, LANES), lambda i: (i, col // LANES))
    return pl.pallas_call(
        _dsa_prep_kernel,
        grid=(t // tm,),
        in_specs=[pl.BlockSpec((tm, 1), lambda i: (i, 0)), vec] + [head_blk(c) for c in head_cols],
        out_specs=[out(ATT_WIDTH), out(IDX_HEADS * IDX_HEAD), out(kvw), out(kvw), out(LANES), out(LANES)],
        out_shape=[jax.ShapeDtypeStruct((t, ATT_WIDTH), bf16),
                   jax.ShapeDtypeStruct((t, IDX_HEADS * IDX_HEAD), bf16),
                   jax.ShapeDtypeStruct((t, kvw), bf16),
                   jax.ShapeDtypeStruct((t, kvw), bf16),
                   jax.ShapeDtypeStruct((t, LANES), bf16),
                   jax.ShapeDtypeStruct((t, LANES), f32)],
        compiler_params=_params("parallel"),
        name="dsa_prep",
    )(pos, freq, *([p] * len(head_cols)))


def _dsa_kernel(q_ref, qi_ref, wi_ref, k_ref, v_ref, ki2_ref, o_ref, score_ref, key_ref, logit_ref,
                *, n_sel, bucket_len):
    tq = q_ref.shape[0]
    seq = k_ref.shape[0]
    bucket = ((pl.program_id(1) + 1) * tq - 1) // bucket_len
    for j in range(seq // bucket_len):
        pl.when(bucket == j)(functools.partial(
            _dsa_body, q_ref, qi_ref, wi_ref, k_ref, v_ref, ki2_ref, o_ref, score_ref, key_ref, logit_ref,
            n_sel=n_sel, s_len=(j + 1) * bucket_len))


def _dsa_body(q_ref, qi_ref, wi_ref, k_ref, v_ref, ki2_ref, o_ref, score_ref, key_ref, logit_ref,
              *, n_sel, s_len):
    tq = q_ref.shape[0]
    q0 = pl.program_id(1) * tq
    lane = lax.broadcasted_iota(jnp.int32, (1, LANES), 1)
    first_half = lane < IDX_HEAD
    ki2 = ki2_ref[0:s_len, :]
    wi = wi_ref[...]

    score = jnp.zeros((tq, s_len), f32)
    for h in range(IDX_HEADS):
        pair = h // 2
        m = first_half if h % 2 == 0 else jnp.logical_not(first_half)
        qm = jnp.where(m, qi_ref[:, pair * LANES:(pair + 1) * LANES], jnp.zeros((), bf16))
        rel = jnp.maximum(_nt(qm, ki2), 0.0)
        w_h = jnp.sum(jnp.where(lane == h, wi, 0.0), axis=-1, keepdims=True)
        score = score + rel * w_h

    group = ATT_HEADS // ATT_KV_HEADS
    for kv in range(ATT_KV_HEADS):
        qg = jnp.concatenate([q_ref[:, (kv * group + g) * LANES:(kv * group + g + 1) * LANES]
                              for g in range(group)], axis=0)
        logit_ref[kv, :, 0:s_len] = _nt(qg, k_ref[0:s_len, kv * LANES:(kv + 1) * LANES]) * (ATT_HEAD ** -0.5)

    q_pos = q0 + lax.broadcasted_iota(jnp.int32, (tq, 1), 0)
    k_pos = lax.broadcasted_iota(jnp.int32, (1, s_len), 1)
    causal = k_pos <= q_pos
    score = jnp.where(causal, score, -1e30) + 0.0
    score_ref[:, 0:s_len] = score

    n8 = s_len // SUBLANES
    bits = pltpu.bitcast(score.T, jnp.int32)
    key_t = jnp.where(bits >= 0, bits, bits ^ jnp.int32(0x7FFFFFFF))
    key_ref[0:n8] = key_t.reshape(n8, SUBLANES, tq)
    int_min = jnp.int32(-2 ** 31)

    def search(i, t_u):
        cand = t_u | lax.shift_right_logical(int_min, i)
        ge = key_ref[0:n8] >= (cand ^ int_min)[None]
        ones = jnp.where(ge, 1.0, 0.0).reshape(n8 // COUNT_CHAINS, COUNT_CHAINS, SUBLANES, tq)
        cnt = jnp.sum(jnp.sum(ones, axis=0), axis=0)
        for shift in (4, 2, 1):
            cnt = cnt + pltpu.roll(cnt, shift, 0)
        return jnp.where(cnt >= n_sel, cand, t_u)

    t_u = lax.fori_loop(0, 32, search, jnp.zeros((SUBLANES, tq), jnp.int32), unroll=4)
    t_key = t_u[0:1, :] ^ int_min
    thr_row = pltpu.bitcast(jnp.where(t_key >= 0, t_key, t_key ^ jnp.int32(0x7FFFFFFF)), f32)
    diag = lax.broadcasted_iota(jnp.int32, (tq, tq), 0) == lax.broadcasted_iota(jnp.int32, (tq, tq), 1)
    thr_col = jnp.sum(jnp.where(diag, thr_row, 0.0), axis=-1, keepdims=True)
    selected = (score_ref[:, 0:s_len] >= thr_col) & causal

    sel_g = jnp.concatenate([selected] * group, axis=0)
    probs, denoms = [], []
    for kv in range(ATT_KV_HEADS):
        s = jnp.where(sel_g, logit_ref[kv, :, 0:s_len], -jnp.inf)
        e = jnp.exp(s - jnp.max(s, axis=-1, keepdims=True))
        denoms.append(jnp.sum(e, axis=-1, keepdims=True))
        probs.append(e.astype(bf16))
    for kv in range(ATT_KV_HEADS):
        o = _mm(probs[kv], v_ref[0:s_len, kv * LANES:(kv + 1) * LANES]) / denoms[kv]
        for g in range(group):
            hq = kv * group + g
            o_ref[:, hq * LANES:(hq + 1) * LANES] = o[g * tq:(g + 1) * tq].astype(o_ref.dtype)


def _dsa(q, qi, wi, k, v, ki2, batch, seq, tq):
    nq = seq // tq
    n_sel = min(TOPK_MAX, seq // 4)
    qblk = lambda width: pl.BlockSpec((tq, width), lambda b, i: (b * nq + i, 0))
    kblk = lambda width: pl.BlockSpec((seq, width), lambda b, i: (b, 0))
    kvw = ATT_KV_HEADS * ATT_HEAD
    bucket_len = max(tq, seq // DSA_BUCKETS)
    return pl.pallas_call(
        functools.partial(_dsa_kernel, n_sel=n_sel, bucket_len=bucket_len),
        grid=(batch, nq),
        in_specs=[qblk(ATT_WIDTH), qblk(IDX_HEADS * IDX_HEAD), qblk(LANES),
                  kblk(kvw), kblk(kvw), kblk(LANES)],
        out_specs=qblk(ATT_WIDTH),
        out_shape=jax.ShapeDtypeStruct((batch * seq, ATT_WIDTH), bf16),
        scratch_shapes=[pltpu.VMEM((tq, seq), f32),
                        pltpu.VMEM((seq // SUBLANES, SUBLANES, tq), jnp.int32),
                        pltpu.VMEM((ATT_KV_HEADS, (ATT_HEADS // ATT_KV_HEADS) * tq, seq), f32)],
        compiler_params=_params("parallel", "arbitrary"),
        name="dsa_attention",
    )(q, qi, wi, k, v, ki2)


def _mix_out_kernel(x_ref, ya_ref, yb_ref, wa_ref, wb_ref, o_ref):
    o_ref[...] = x_ref[...] + _mm(ya_ref[...], wa_ref[...]) + _mm(yb_ref[...], wb_ref[...])


def _mix_out(x, ya, yb, w, tm, tn):
    t, d = x.shape
    ka = ya.shape[1]
    kb = yb.shape[1]
    return pl.pallas_call(
        _mix_out_kernel,
        grid=(t // tm, d // tn),
        in_specs=[pl.BlockSpec((tm, tn), lambda i, j: (i, j)),
                  pl.BlockSpec((tm, ka), lambda i, j: (i, 0)),
                  pl.BlockSpec((tm, kb), lambda i, j: (i, 0)),
                  pl.BlockSpec((ka, tn), lambda i, j: (0, j)),
                  pl.BlockSpec((kb, tn), lambda i, j: (ka // kb, j))],
        out_specs=pl.BlockSpec((tm, tn), lambda i, j: (i, j)),
        out_shape=jax.ShapeDtypeStruct((t, d), f32),
        compiler_params=_params("parallel", "arbitrary"),
        name="mix_out",
    )(x, ya, yb, w, w)


def _cross_kernel(h_ref, g_ref, kv_ref, wq_ref, wo_ref, o_ref):
    h = h_ref[...]
    hn = _rms(h, g_ref[...]).astype(bf16)
    q = _mm(hn, wq_ref[...]).astype(bf16)
    width = CROSS_HEADS * CROSS_HEAD
    cols = lambda hd, base=0: slice(base + hd * CROSS_HEAD, base + (hd + 1) * CROSS_HEAD)
    logits = [_nt(q[:, cols(hd)], kv_ref[:, cols(hd)]) * (CROSS_HEAD ** -0.5) for hd in range(CROSS_HEADS)]
    exps = [jnp.exp(s - jnp.max(s, axis=-1, keepdims=True)) for s in logits]
    outs = [(_mm(e.astype(bf16), kv_ref[:, cols(hd, width)]) / jnp.sum(e, axis=-1, keepdims=True)).astype(bf16)
            for hd, e in enumerate(exps)]
    o = jnp.concatenate(outs, axis=-1)
    o_ref[...] = h + _mm(o, wo_ref[...])


def _cross(h, g, kv, wq, wo, batch, seq, n_mem, tm):
    d = h.shape[1]
    ns = seq // tm
    width = CROSS_HEADS * CROSS_HEAD
    return pl.pallas_call(
        _cross_kernel,
        grid=(batch, ns),
        in_specs=[pl.BlockSpec((tm, d), lambda b, i: (b * ns + i, 0)),
                  pl.BlockSpec((1, d), lambda b, i: (0, 0)),
                  pl.BlockSpec((n_mem, 2 * width), lambda b, i: (b, 0)),
                  pl.BlockSpec((d, width), lambda b, i: (0, 0)),
                  pl.BlockSpec((width, d), lambda b, i: (0, 0))],
        out_specs=pl.BlockSpec((tm, d), lambda b, i: (b * ns + i, 0)),
        out_shape=jax.ShapeDtypeStruct(h.shape, f32),
        compiler_params=_params("parallel", "arbitrary"),
        name="cross_attention",
    )(h, g, kv, wq, wo)


def _mlp_kernel(h_ref, g_ref, wu_ref, wd_ref, gf_ref, o_ref, hn_ref, *, final_norm):
    j = pl.program_id(1)

    @pl.when(j == 0)
    def _():
        h = h_ref[...]
        hn_ref[...] = _rms(h, g_ref[...]).astype(bf16)
        o_ref[...] = h

    u = jnp.maximum(_mm(hn_ref[...], wu_ref[...]), 0.0)
    o_ref[...] += _mm((u * u).astype(bf16), wd_ref[...])

    if final_norm:
        @pl.when(j == pl.num_programs(1) - 1)
        def _():
            o_ref[...] = _rms(o_ref[...], gf_ref[...])


def _mlp(h, g, wu, wd, gf, final_norm, tm, tf):
    t, d = h.shape
    dff = wu.shape[1]
    return pl.pallas_call(
        functools.partial(_mlp_kernel, final_norm=final_norm),
        grid=(t // tm, dff // tf),
        in_specs=[pl.BlockSpec((tm, d), lambda i, j: (i, 0)),
                  pl.BlockSpec((1, d), lambda i, j: (0, 0)),
                  pl.BlockSpec((d, tf), lambda i, j: (0, j)),
                  pl.BlockSpec((tf, d), lambda i, j: (j, 0)),
                  pl.BlockSpec((1, d), lambda i, j: (0, 0))],
        out_specs=pl.BlockSpec((tm, d), lambda i, j: (i, 0)),
        out_shape=jax.ShapeDtypeStruct((t, d), f32),
        scratch_shapes=[pltpu.VMEM((tm, d), bf16)],
        compiler_params=_params("parallel", "arbitrary"),
        name="mlp_final_norm",
    )(h, g, wu, wd, gf)


def _pack_in_proj(w_in, mu):
    d = w_in.shape[0]
    zeros = lambda n: jnp.zeros((d, n), w_in.dtype)
    w = jnp.concatenate([w_in[:, :RWKV_COLS], zeros(COL_Q - RWKV_COLS),
                         w_in[:, RWKV_COLS:], zeros(IN_COLS_PACKED - COL_Q - ATT_COLS)], axis=1).astype(bf16)
    mu_lo = jnp.concatenate([mu[COL_LORA:RWKV_COLS], jnp.zeros((COL_Q - RWKV_COLS,), mu.dtype)])
    mus = (mu[0:1024][None], mu[1024:2048][None], mu[2048:3072][None], mu_lo[None])
    return w, mus


def _rope_freq():
    def inv_freq(head):
        half = head // 8
        return ROPE_THETA ** (-jnp.arange(half, dtype=f32) / half)
    lane = jnp.arange(LANES)
    att = inv_freq(ATT_HEAD)[lane % (ATT_HEAD // 8)]
    idx = inv_freq(IDX_HEAD)[lane % (IDX_HEAD // 8)]
    in_idx = (lane >= IDX_HEAD) & (lane < IDX_HEAD + IDX_HEAD // 4)
    return jnp.where(lane < ATT_HEAD // 4, att, jnp.where(in_idx, idx, 0.0)).astype(f32)[None]


def kernel(x, mem, positions, norm_mix, w_in, rwkv_mu, w_decay0, w_decay_up, a0, a_up, g_up, k_k, k_a, r_k,
           lnx_w, lnx_b, w_mix_out, norm_cross, norm_mem, w_q_cross, w_kv_cross, w_o_cross, norm_mlp,
           w_up, w_down, norm_final):
    batch, seq, d = x.shape
    n_mem = mem.shape[1]
    t = batch * seq
    depth = w_in.shape[0]
    tm = min(512, t)
    h = x.reshape(t, d)
    pos = positions.reshape(t, 1).astype(f32)
    mem2 = mem.reshape(batch * n_mem, d)
    row = lambda vct: vct.reshape(1, -1)

    for l in range(depth):
        w_in_p, mus = _pack_in_proj(w_in[l], rwkv_mu[l])
        zpad = jnp.zeros((DECAY_LORA, RWKV_WIDTH), f32)
        w_wa = jnp.concatenate([jnp.concatenate([w_decay_up[l], zpad], axis=1),
                                jnp.concatenate([zpad, a_up[l]], axis=1)], axis=0).astype(bf16)
        g_up_p = jnp.concatenate([g_up[l], jnp.zeros((LORA_BLOCK - LANES - GATE_LORA, RWKV_WIDTH), f32)],
                                 axis=0).astype(bf16)

        p = _norm_matmul(h, row(norm_mix[l]), w_in_p, f32, min(1024, t), 1024)
        y_rwkv = _rwkv(p, batch, seq, mus, row(w_decay0[l]), row(a0[l]), row(k_k[l]), row(k_a[l]),
                       row(lnx_w[l]), row(lnx_b[l]), row(r_k[l]), w_wa, g_up_p)
        q, qi, k_att, v_att, ki2, wi = _dsa_prep(p, pos, _rope_freq(), tm)
        y_att = _dsa(q, qi, wi, k_att, v_att, ki2, batch, seq, min(DSA_QUERY_BLOCK, seq))
        h = _mix_out(h, y_rwkv, y_att, w_mix_out[l].astype(bf16), tm, d)

        kv = _norm_matmul(mem2, row(norm_mem[l]), w_kv_cross[l].astype(bf16), bf16, min(512, batch * n_mem), 1024)
        h = _cross(h, row(norm_cross[l]), kv, w_q_cross[l].astype(bf16), w_o_cross[l].astype(bf16),
                   batch, seq, n_mem, tm)
        h = _mlp(h, row(norm_mlp[l]), w_up[l].astype(bf16), w_down[l].astype(bf16), row(norm_final),
                 l == depth - 1, tm, 1024)
    return h.reshape(batch, seq, d)
```

```python
import functools

import jax
import jax.numpy as jnp
from jax import lax
from jax.experimental import pallas as pl
from jax.experimental.pallas import tpu as pltpu

f32 = jnp.float32
bf16 = jnp.bfloat16

D_MODEL = 2048
RWKV_WIDTH = 1024
RWKV_HEAD = 64
DECAY_LORA = 64
AAA_LORA = 64
GATE_LORA = 160
RWKV_LN_EPS = 64e-5
ATT_WIDTH = 1024
ATT_HEAD = 128
ATT_HEADS = 8
ATT_KV_HEADS = 2
IDX_HEADS = 16
IDX_HEAD = 64
TOPK_MAX = 256
ROPE_THETA = 500000.0
CROSS_HEADS = 4
CROSS_HEAD = 128
D_FF = 4 * D_MODEL
NORM_EPS = 1e-5

LANES = 128
SUBLANES = 8
VMEM_LIMIT_BYTES = 56 * 1024 * 1024

COL_R, COL_K, COL_V, COL_LORA = 0, 1024, 2048, 3072
LORA_BLOCK = 384
RWKV_COLS, ATT_COLS = 3360, 2640
COL_Q = COL_LORA + LORA_BLOCK
COL_KATT = COL_Q + ATT_WIDTH
COL_VATT = COL_KATT + ATT_KV_HEADS * ATT_HEAD
COL_QI = COL_VATT + ATT_KV_HEADS * ATT_HEAD
COL_KIWI = COL_QI + IDX_HEADS * IDX_HEAD
IN_COLS_PACKED = COL_KIWI + LANES
RWKV_CHUNK = 64
_EXP_NEG_HALF = 0.6065306597126334
RWKV_BATCH_GROUP = 2
DSA_QUERY_BLOCK = 128
COUNT_CHAINS = 8
DSA_BUCKETS = 8


def _nt(a, b):
    return lax.dot_general(a, b, (((1,), (1,)), ((), ())), preferred_element_type=f32)


def _mm(a, b):
    return jnp.dot(a, b, preferred_element_type=f32)


def _rms(x, g):
    ms = jnp.mean(x * x, axis=-1, keepdims=True)
    return x * lax.rsqrt(ms + NORM_EPS) * g


def _params(*sem):
    return pltpu.CompilerParams(dimension_semantics=sem, vmem_limit_bytes=VMEM_LIMIT_BYTES)


def _norm_matmul_kernel(x_ref, g_ref, w_ref, o_ref, xn_ref, *, w_is_transposed):
    @pl.when(pl.program_id(1) == 0)
    def _():
        xn_ref[...] = _rms(x_ref[...], g_ref[...]).astype(bf16)

    dot = _nt if w_is_transposed else _mm
    o_ref[...] = dot(xn_ref[...], w_ref[...]).astype(o_ref.dtype)


def _norm_matmul(x, g, w, out_dtype, tm, tn, w_is_transposed=False):
    t, d = x.shape
    n = w.shape[0] if w_is_transposed else w.shape[1]
    w_spec = (pl.BlockSpec((tn, d), lambda i, j: (j, 0)) if w_is_transposed
              else pl.BlockSpec((d, tn), lambda i, j: (0, j)))
    return pl.pallas_call(
        functools.partial(_norm_matmul_kernel, w_is_transposed=w_is_transposed),
        grid=(t // tm, n // tn),
        in_specs=[
            pl.BlockSpec((tm, d), lambda i, j: (i, 0)),
            pl.BlockSpec((1, d), lambda i, j: (0, 0)),
            w_spec,
        ],
        out_specs=pl.BlockSpec((tm, tn), lambda i, j: (i, j)),
        out_shape=jax.ShapeDtypeStruct((t, n), out_dtype),
        scratch_shapes=[pltpu.VMEM((tm, d), bf16)],
        compiler_params=_params("parallel", "arbitrary"),
        name="norm_matmul",
    )(x, g, w)


def _seg_sum(x, first_half):
    s_a = jnp.sum(jnp.where(first_half, x, 0.0), axis=-1, keepdims=True)
    s_b = jnp.sum(jnp.where(first_half, 0.0, x), axis=-1, keepdims=True)
    return jnp.where(first_half, s_a, s_b)


def _rwkv_kernel(r_ref, k_ref, v_ref, lo_ref, mur_ref, muk_ref, muv_ref, mulo_ref,
                 w0_ref, a0_ref, kkw_ref, kaw_ref, lnw_ref, lnb_ref, rk_ref, wwa_ref, gup_ref,
                 y_ref, st_ref, pr_ref, pk_ref, pv_ref, plo_ref):
    G, C = r_ref.shape[0], r_ref.shape[1]
    n_pairs = RWKV_WIDTH // LANES

    @pl.when(pl.program_id(1) == 0)
    def _():
        st_ref[...] = jnp.zeros_like(st_ref)
        pr_ref[...] = jnp.zeros_like(pr_ref)
        pk_ref[...] = jnp.zeros_like(pk_ref)
        pv_ref[...] = jnp.zeros_like(pv_ref)
        plo_ref[...] = jnp.zeros_like(plo_ref)

    row = lax.broadcasted_iota(jnp.int32, (C, 1), 0)
    lane = lax.broadcasted_iota(jnp.int32, (1, LANES), 1)
    first_half = lane < RWKV_HEAD
    ti = lax.broadcasted_iota(jnp.int32, (C, C), 0)
    tj = lax.broadcasted_iota(jnp.int32, (C, C), 1)
    tri_b = jnp.where(ti >= tj, 1.0, 0.0).astype(bf16)
    t2 = lax.broadcasted_iota(jnp.int32, (C, 2 * C), 0)
    j2 = lax.broadcasted_iota(jnp.int32, (C, 2 * C), 1) % C
    tri2_strict = t2 > j2
    eye2 = jnp.where(t2 == j2, 1.0, 0.0)
    t4 = lax.broadcasted_iota(jnp.int32, (C, 4 * C), 0)
    j4 = lax.broadcasted_iota(jnp.int32, (C, 4 * C), 1) % C
    tri4_incl = t4 >= j4
    vi = lax.broadcasted_iota(jnp.int32, (LANES, LANES), 0)
    vj = lax.broadcasted_iota(jnp.int32, (LANES, LANES), 1)
    same_head = (vi < RWKV_HEAD) == (vj < RWKV_HEAD)

    def head_rows(x):
        xb = x.astype(bf16)
        zero = jnp.zeros((), bf16)
        return jnp.concatenate([jnp.where(first_half, xb, zero), jnp.where(first_half, zero, xb)], axis=0)

    def shifted(x_ref, prev_ref, mu_ref, g):
        x = x_ref[g]
        prev = jnp.where(row == 0, prev_ref[g], pltpu.roll(x, 1, 0))
        prev_ref[g] = x[C - 1:C, :]
        return x + (prev - x) * mu_ref[...]

    units = [(g, p) for g in range(G) for p in range(n_pairs)]
    lanes_of = lambda p: slice(p * LANES, (p + 1) * LANES)
    v_all, gate_all, rkk_all = [], [], []
    a_t, b_t, k_t, r_t, g_end, s0 = [], [], [], [], [], []
    for g in range(G):
        r = shifted(r_ref, pr_ref, mur_ref, g)
        k = shifted(k_ref, pk_ref, muk_ref, g)
        v = shifted(v_ref, pv_ref, muv_ref, g)
        lo = shifted(lo_ref, plo_ref, mulo_ref, g)
        wa_in = lo[:, :LANES]
        wa_in = jnp.where(first_half, 1.0 - 2.0 / (1.0 + jnp.exp(2.0 * wa_in)), wa_in).astype(bf16)
        wa = _mm(wa_in, wwa_ref[...])
        logdecay = -_EXP_NEG_HALF / (1.0 + jnp.exp(-(w0_ref[...] + wa[:, :RWKV_WIDTH])))
        a = 1.0 / (1.0 + jnp.exp(-(a0_ref[...] + wa[:, RWKV_WIDTH:])))
        gate_all.append(_mm((1.0 / (1.0 + jnp.exp(-lo[:, LANES:]))).astype(bf16), gup_ref[...]))
        ld_hi = logdecay.astype(bf16)
        ld_r1 = logdecay - ld_hi.astype(f32)
        ld_mid = ld_r1.astype(bf16)
        ld_lo = (ld_r1 - ld_mid.astype(f32)).astype(bf16)
        cum = _mm(tri_b, ld_hi) + _mm(tri_b, ld_mid) + _mm(tri_b, ld_lo)
        kk = k * kkw_ref[...]
        k2 = k * (1.0 + (a - 1.0) * kaw_ref[...])
        rkk_all.append(r * k2 * rk_ref[...])
        v_all.append(v)
        ecum = jnp.exp(cum)
        encum = jnp.exp(-cum)
        ecum_prev = jnp.exp(cum - logdecay)
        for p in range(n_pairs):
            sl = lanes_of(p)
            kkp = kk[:, sl]
            kkn = kkp * lax.rsqrt(jnp.maximum(_seg_sum(kkp * kkp, first_half), 1e-24))
            a_t.append(-kkn * ecum_prev[:, sl])
            b_t.append(kkn * a[:, sl] * encum[:, sl])
            k_t.append(k2[:, sl] * encum[:, sl])
            r_t.append(r[:, sl] * ecum[:, sl])
            g_end.append(ecum[C - 1:C, sl])
            s0.append(st_ref[g * n_pairs + p])

    n = len(units)
    vp = [v_all[g][:, lanes_of(p)] for g, p in units]
    v_rows = [head_rows(x) for x in vp]
    ar = [jnp.concatenate([a_t[i], r_t[i]], axis=0).astype(bf16) for i in range(n)]
    ar_all = [_nt(ar[i], jnp.concatenate([s0[i].astype(bf16), head_rows(b_t[i]), head_rows(k_t[i])], axis=0))
              for i in range(n)]
    ar0 = [x[:, :LANES] for x in ar_all]
    gram = [x[:, LANES:] for x in ar_all]
    l_b = [jnp.where(tri2_strict, x[:C, :2 * C], 0.0) for x in gram]
    l_k = [jnp.where(tri2_strict, x[:C, 2 * C:], 0.0).astype(bf16) for x in gram]
    m_bk = [jnp.where(tri4_incl, x[C:, :], 0.0).astype(bf16) for x in gram]
    w = [ar0[i][:C] + _mm(l_k[i], v_rows[i]) for i in range(n)]
    inv = [eye2 + x for x in l_b]
    l_pow_rows = [head_rows(x) for x in l_b]
    l_pow = [_mm(x.astype(bf16), y) for x, y in zip(l_b, l_pow_rows)]
    n_steps = C.bit_length() - 2
    for step in range(n_steps):
        l_pow_rows = [head_rows(x) for x in l_pow]
        if step < n_steps - 1:
            both = [_mm(jnp.concatenate([x, y], axis=0).astype(bf16), z)
                    for x, y, z in zip(inv, l_pow, l_pow_rows)]
            inv = [x + y[:C] for x, y in zip(inv, both)]
            l_pow = [y[C:] for y in both]
        else:
            inv = [x + _mm(x.astype(bf16), z) for x, z in zip(inv, l_pow_rows)]
    u = [_mm(inv[i].astype(bf16), head_rows(w[i])) for i in range(n)]
    y_all = [ar0[i][C:] + _mm(m_bk[i], jnp.concatenate([head_rows(u[i]), v_rows[i]], axis=0))
             for i in range(n)]
    upd = [_mm(jnp.concatenate([u[i], vp[i]], axis=0).T.astype(bf16),
               (jnp.concatenate([b_t[i], k_t[i]], axis=0) * g_end[i]).astype(bf16)) for i in range(n)]
    for i, (g, p) in enumerate(units):
        sl = lanes_of(p)
        st_ref[g * n_pairs + p] = s0[i] * g_end[i] + jnp.where(same_head, upd[i], 0.0)
        y = y_all[i]
        mean = _seg_sum(y, first_half) * (1.0 / RWKV_HEAD)
        yc = y - mean
        var = _seg_sum(yc * yc, first_half) * (1.0 / RWKV_HEAD)
        yn = yc * lax.rsqrt(var + RWKV_LN_EPS) * lnw_ref[:, sl] + lnb_ref[:, sl]
        bonus = _seg_sum(rkk_all[g][:, sl], first_half) * vp[i]
        y_ref[g, :, sl] = ((yn + bonus) * gate_all[g][:, sl]).astype(y_ref.dtype)


def _rwkv(p, batch, seq, mu, w0, a0, k_k, k_a, lnx_w, lnx_b, r_k, w_wa, g_up_p):
    C = RWKV_CHUNK
    G = RWKV_BATCH_GROUP if batch % RWKV_BATCH_GROUP == 0 else 1
    W = RWKV_WIDTH
    p3 = p.reshape(batch, seq, p.shape[-1])
    row_block = lambda width, col: pl.BlockSpec((G, C, width), lambda b, c: (b, c, col // width))
    vec = lambda width: pl.BlockSpec((1, width), lambda b, c: (0, 0))
    full = lambda arr: pl.BlockSpec(arr.shape, lambda b, c: (0, 0))
    mu_r, mu_k, mu_v, mu_lo = mu
    y = pl.pallas_call(
        _rwkv_kernel,
        grid=(batch // G, seq // C),
        in_specs=[row_block(W, COL_R), row_block(W, COL_K), row_block(W, COL_V),
                  row_block(LORA_BLOCK, COL_LORA),
                  vec(W), vec(W), vec(W), vec(LORA_BLOCK),
                  vec(W), vec(W), vec(W), vec(W), vec(W), vec(W), vec(W),
                  full(w_wa), full(g_up_p)],
        out_specs=pl.BlockSpec((G, C, W), lambda b, c: (b, c, 0)),
        out_shape=jax.ShapeDtypeStruct((batch, seq, W), bf16),
        scratch_shapes=[pltpu.VMEM((G * W // LANES, LANES, LANES), f32),
                        pltpu.VMEM((G, 1, W), f32), pltpu.VMEM((G, 1, W), f32), pltpu.VMEM((G, 1, W), f32),
                        pltpu.VMEM((G, 1, LORA_BLOCK), f32)],
        compiler_params=_params("parallel", "arbitrary"),
        name="rwkv7_chunked",
    )(p3, p3, p3, p3, mu_r, mu_k, mu_v, mu_lo, w0, a0, k_k, k_a, lnx_w, lnx_b, r_k, w_wa, g_up_p)
    return y.reshape(batch * seq, W)


def _rope(x, cos, sin_lo, sin_hi, half):
    n = x.shape[-1]
    return x * cos + pltpu.roll(x, n - half, 1) * sin_lo + pltpu.roll(x, half, 1) * sin_hi


def _dsa_prep_kernel(pos_ref, freq_ref, *refs):
    n_qi = IDX_HEADS * IDX_HEAD // LANES
    q_refs, refs = refs[:ATT_HEADS], refs[ATT_HEADS:]
    qi_refs, refs = refs[:n_qi], refs[n_qi:]
    ka_refs, refs = refs[:ATT_KV_HEADS], refs[ATT_KV_HEADS:]
    va_refs, refs = refs[:ATT_KV_HEADS], refs[ATT_KV_HEADS:]
    kiwi_ref, qo_ref, qio_ref, ko_ref, vo_ref, ki2_ref, wi_ref = refs
    pos = pos_ref[...]
    lane = lax.broadcasted_iota(jnp.int32, (1, LANES), 1)
    ang = pos * freq_ref[...]
    cos_t, sin_t = jnp.cos(ang), jnp.sin(ang)

    def tables(cos, sin, head, half):
        in_head = lane % head
        sin_lo = jnp.where(in_head < half, -sin, 0.0)
        sin_hi = jnp.where((in_head >= half) & (in_head < 2 * half), sin, 0.0)
        return cos, sin_lo, sin_hi

    att_lanes = lane < ATT_HEAD // 4
    upper = lane >= IDX_HEAD
    ca, sla, sha = tables(jnp.where(att_lanes, cos_t, 1.0), jnp.where(att_lanes, sin_t, 0.0),
                          ATT_HEAD, ATT_HEAD // 8)
    ci, sli, shi = tables(jnp.where(upper, cos_t, pltpu.roll(cos_t, IDX_HEAD, 1)),
                          jnp.where(upper, sin_t, pltpu.roll(sin_t, IDX_HEAD, 1)),
                          IDX_HEAD, IDX_HEAD // 8)
    lanes_of = lambda h: slice(h * LANES, (h + 1) * LANES)
    for h in range(ATT_HEADS):
        qo_ref[:, lanes_of(h)] = _rope(q_refs[h][...], ca, sla, sha, ATT_HEAD // 8).astype(bf16)
    for h in range(n_qi):
        qio_ref[:, lanes_of(h)] = _rope(qi_refs[h][...], ci, sli, shi, IDX_HEAD // 8).astype(bf16)
    for h in range(ATT_KV_HEADS):
        ko_ref[:, lanes_of(h)] = _rope(ka_refs[h][...], ca, sla, sha, ATT_HEAD // 8).astype(bf16)
        vo_ref[:, lanes_of(h)] = va_refs[h][...].astype(bf16)
    kiwi = kiwi_ref[...]
    ki_only = jnp.where(lane < IDX_HEAD, kiwi, 0.0)
    ki = _rope(ki_only, ci, sli, shi, IDX_HEAD // 8)
    ki2_ref[...] = (ki + pltpu.roll(ki, IDX_HEAD, 1)).astype(bf16)
    wi = pltpu.roll(kiwi, LANES - IDX_HEAD, 1)
    wi_ref[...] = jnp.where(lane < IDX_HEADS, wi, 0.0) * (IDX_HEADS ** -0.5 * IDX_HEAD ** -0.5)


def _dsa_prep(p, pos, freq, tm):
    t = p.shape[0]
    out = lambda width: pl.BlockSpec((tm, width), lambda i: (i, 0))
    vec = pl.BlockSpec((1, LANES), lambda i: (0, 0))
    kvw = ATT_KV_HEADS * ATT_HEAD
    head_cols = ([COL_Q + h * LANES for h in range(ATT_HEADS)]
                 + [COL_QI + h * LANES for h in range(IDX_HEADS * IDX_HEAD // LANES)]
                 + [COL_KATT + h * LANES for h in range(ATT_KV_HEADS)]
                 + [COL_VATT + h * LANES for h in range(ATT_KV_HEADS)]
                 + [COL_KIWI])
    head_blk = lambda col: pl.BlockSpec((tm, LANES), lambda i: (i, col // LANES))
    return pl.pallas_call(
        _dsa_prep_kernel,
        grid=(t // tm,),
        in_specs=[pl.BlockSpec((tm, 1), lambda i: (i, 0)), vec] + [head_blk(c) for c in head_cols],
        out_specs=[out(ATT_WIDTH), out(IDX_HEADS * IDX_HEAD), out(kvw), out(kvw), out(LANES), out(LANES)],
        out_shape=[jax.ShapeDtypeStruct((t, ATT_WIDTH), bf16),
                   jax.ShapeDtypeStruct((t, IDX_HEADS * IDX_HEAD), bf16),
                   jax.ShapeDtypeStruct((t, kvw), bf16),
                   jax.ShapeDtypeStruct((t, kvw), bf16),
                   jax.ShapeDtypeStruct((t, LANES), bf16),
                   jax.ShapeDtypeStruct((t, LANES), f32)],
        compiler_params=_params("parallel"),
        name="dsa_prep",
    )(pos, freq, *([p] * len(head_cols)))


def _dsa_kernel(q_ref, qi_ref, wi_ref, k_ref, v_ref, ki2_ref, o_ref, score_ref, key_ref, logit_ref,
                *, n_sel, bucket_len):
    tq = q_ref.shape[0]
    seq = k_ref.shape[0]
    bucket = ((pl.program_id(1) + 1) * tq - 1) // bucket_len
    for j in range(seq // bucket_len):
        pl.when(bucket == j)(functools.partial(
            _dsa_body, q_ref, qi_ref, wi_ref, k_ref, v_ref, ki2_ref, o_ref, score_ref, key_ref, logit_ref,
            n_sel=n_sel, s_len=(j + 1) * bucket_len))


def _dsa_body(q_ref, qi_ref, wi_ref, k_ref, v_ref, ki2_ref, o_ref, score_ref, key_ref, logit_ref,
              *, n_sel, s_len):
    tq = q_ref.shape[0]
    q0 = pl.program_id(1) * tq
    lane = lax.broadcasted_iota(jnp.int32, (1, LANES), 1)
    first_half = lane < IDX_HEAD
    ki2 = ki2_ref[0:s_len, :]
    wi = wi_ref[...]

    score = jnp.zeros((tq, s_len), f32)
    for h in range(IDX_HEADS):
        pair = h // 2
        m = first_half if h % 2 == 0 else jnp.logical_not(first_half)
        qm = jnp.where(m, qi_ref[:, pair * LANES:(pair + 1) * LANES], jnp.zeros((), bf16))
        rel = jnp.maximum(_nt(qm, ki2), 0.0)
        w_h = jnp.sum(jnp.where(lane == h, wi, 0.0), axis=-1, keepdims=True)
        score = score + rel * w_h

    group = ATT_HEADS // ATT_KV_HEADS
    for kv in range(ATT_KV_HEADS):
        qg = jnp.concatenate([q_ref[:, (kv * group + g) * LANES:(kv * group + g + 1) * LANES]
                              for g in range(group)], axis=0)
        logit_ref[kv, :, 0:s_len] = _nt(qg, k_ref[0:s_len, kv * LANES:(kv + 1) * LANES]) * (ATT_HEAD ** -0.5)

    q_pos = q0 + lax.broadcasted_iota(jnp.int32, (tq, 1), 0)
    k_pos = lax.broadcasted_iota(jnp.int32, (1, s_len), 1)
    causal = k_pos <= q_pos
    score = jnp.where(causal, score, -1e30) + 0.0
    score_ref[:, 0:s_len] = score

    n8 = s_len // SUBLANES
    bits = pltpu.bitcast(score.T, jnp.int32)
    key_t = jnp.where(bits >= 0, bits, bits ^ jnp.int32(0x7FFFFFFF))
    key_ref[0:n8] = key_t.reshape(n8, SUBLANES, tq)
    int_min = jnp.int32(-2 ** 31)

    def search(i, t_u):
        cand = t_u | lax.shift_right_logical(int_min, i)
        ge = key_ref[0:n8] >= (cand ^ int_min)[None]
        ones = jnp.where(ge, 1.0, 0.0).reshape(n8 // COUNT_CHAINS, COUNT_CHAINS, SUBLANES, tq)
        cnt = jnp.sum(jnp.sum(ones, axis=0), axis=0)
        for shift in (4, 2, 1):
            cnt = cnt + pltpu.roll(cnt, shift, 0)
        return jnp.where(cnt >= n_sel, cand, t_u)

    t_u = lax.fori_loop(0, 32, search, jnp.zeros((SUBLANES, tq), jnp.int32), unroll=4)
    t_key = t_u[0:1, :] ^ int_min
    thr_row = pltpu.bitcast(jnp.where(t_key >= 0, t_key, t_key ^ jnp.int32(0x7FFFFFFF)), f32)
    diag = lax.broadcasted_iota(jnp.int32, (tq, tq), 0) == lax.broadcasted_iota(jnp.int32, (tq, tq), 1)
    thr_col = jnp.sum(jnp.where(diag, thr_row, 0.0), axis=-1, keepdims=True)
    selected = (score_ref[:, 0:s_len] >= thr_col) & causal

    sel_g = jnp.concatenate([selected] * group, axis=0)
    probs, denoms = [], []
    for kv in range(ATT_KV_HEADS):
        s = jnp.where(sel_g, logit_ref[kv, :, 0:s_len], -jnp.inf)
        e = jnp.exp(s - jnp.max(s, axis=-1, keepdims=True))
        denoms.append(jnp.sum(e, axis=-1, keepdims=True))
        probs.append(e.astype(bf16))
    for kv in range(ATT_KV_HEADS):
        o = _mm(probs[kv], v_ref[0:s_len, kv * LANES:(kv + 1) * LANES]) / denoms[kv]
        for g in range(group):
            hq = kv * group + g
            o_ref[:, hq * LANES:(hq + 1) * LANES] = o[g * tq:(g + 1) * tq].astype(o_ref.dtype)


def _dsa(q, qi, wi, k, v, ki2, batch, seq, tq):
    nq = seq // tq
    n_sel = min(TOPK_MAX, seq // 4)
    qblk = lambda width: pl.BlockSpec((tq, width), lambda b, i: (b * nq + i, 0))
    kblk = lambda width: pl.BlockSpec((seq, width), lambda b, i: (b, 0))
    kvw = ATT_KV_HEADS * ATT_HEAD
    bucket_len = max(tq, seq // DSA_BUCKETS)
    return pl.pallas_call(
        functools.partial(_dsa_kernel, n_sel=n_sel, bucket_len=bucket_len),
        grid=(batch, nq),
        in_specs=[qblk(ATT_WIDTH), qblk(IDX_HEADS * IDX_HEAD), qblk(LANES),
                  kblk(kvw), kblk(kvw), kblk(LANES)],
        out_specs=qblk(ATT_WIDTH),
        out_shape=jax.ShapeDtypeStruct((batch * seq, ATT_WIDTH), bf16),
        scratch_shapes=[pltpu.VMEM((tq, seq), f32),
                        pltpu.VMEM((seq // SUBLANES, SUBLANES, tq), jnp.int32),
                        pltpu.VMEM((ATT_KV_HEADS, (ATT_HEADS // ATT_KV_HEADS) * tq, seq), f32)],
        compiler_params=_params("parallel", "arbitrary"),
        name="dsa_attention",
    )(q, qi, wi, k, v, ki2)


def _mix_cross_kernel(x_ref, ya_ref, yb_ref, wa_ref, wb_ref, g_ref, kv_ref, wq_ref, wo_ref, o_ref):
    h = x_ref[...] + _mm(ya_ref[...], wa_ref[...]) + _mm(yb_ref[...], wb_ref[...])
    hn = _rms(h, g_ref[...]).astype(bf16)
    q = _mm(hn, wq_ref[...]).astype(bf16)
    width = CROSS_HEADS * CROSS_HEAD
    cols = lambda hd, base=0: slice(base + hd * CROSS_HEAD, base + (hd + 1) * CROSS_HEAD)
    logits = [_nt(q[:, cols(hd)], kv_ref[:, cols(hd)]) * (CROSS_HEAD ** -0.5) for hd in range(CROSS_HEADS)]
    exps = [jnp.exp(s - jnp.max(s, axis=-1, keepdims=True)) for s in logits]
    outs = [(_mm(e.astype(bf16), kv_ref[:, cols(hd, width)]) / jnp.sum(e, axis=-1, keepdims=True)).astype(bf16)
            for hd, e in enumerate(exps)]
    o = jnp.concatenate(outs, axis=-1)
    o_ref[...] = h + _mm(o, wo_ref[...])


def _mix_cross(x, ya, yb, w_mix, g, kv, wq, wo, batch, seq, n_mem, tm):
    d = x.shape[1]
    ka, kb = ya.shape[1], yb.shape[1]
    ns = seq // tm
    width = CROSS_HEADS * CROSS_HEAD
    rows = lambda cols: pl.BlockSpec((tm, cols), lambda b, i: (b * ns + i, 0))
    const = lambda shape, idx: pl.BlockSpec(shape, lambda b, i: idx, pipeline_mode=pl.Buffered(1))
    return pl.pallas_call(
        _mix_cross_kernel,
        grid=(batch, ns),
        in_specs=[rows(d), rows(ka), rows(kb),
                  const((ka, d), (0, 0)), const((kb, d), (ka // kb, 0)),
                  pl.BlockSpec((1, d), lambda b, i: (0, 0)),
                  pl.BlockSpec((n_mem, 2 * width), lambda b, i: (b, 0)),
                  const((d, width), (0, 0)), const((width, d), (0, 0))],
        out_specs=rows(d),
        out_shape=jax.ShapeDtypeStruct(x.shape, f32),
        compiler_params=_params("parallel", "arbitrary"),
        name="mix_out_cross_attention",
    )(x, ya, yb, w_mix, w_mix, g, kv, wq, wo)


def _mlp_kernel(h_ref, g_ref, wu_ref, wd_ref, gf_ref, o_ref, hn_ref, *, final_norm):
    j = pl.program_id(1)

    @pl.when(j == 0)
    def _():
        h = h_ref[...]
        hn_ref[...] = _rms(h, g_ref[...]).astype(bf16)
        o_ref[...] = h

    u = jnp.maximum(_mm(hn_ref[...], wu_ref[...]), 0.0)
    o_ref[...] += _mm((u * u).astype(bf16), wd_ref[...])

    if final_norm:
        @pl.when(j == pl.num_programs(1) - 1)
        def _():
            o_ref[...] = _rms(o_ref[...], gf_ref[...])


def _mlp(h, g, wu, wd, gf, final_norm, tm, tf):
    t, d = h.shape
    dff = wu.shape[1]
    return pl.pallas_call(
        functools.partial(_mlp_kernel, final_norm=final_norm),
        grid=(t // tm, dff // tf),
        in_specs=[pl.BlockSpec((tm, d), lambda i, j: (i, 0)),
                  pl.BlockSpec((1, d), lambda i, j: (0, 0)),
                  pl.BlockSpec((d, tf), lambda i, j: (0, j)),
                  pl.BlockSpec((tf, d), lambda i, j: (j, 0)),
                  pl.BlockSpec((1, d), lambda i, j: (0, 0))],
        out_specs=pl.BlockSpec((tm, d), lambda i, j: (i, 0)),
        out_shape=jax.ShapeDtypeStruct((t, d), f32),
        scratch_shapes=[pltpu.VMEM((tm, d), bf16)],
        compiler_params=_params("parallel", "arbitrary"),
        name="mlp_final_norm",
    )(h, g, wu, wd, gf)


def _pack_in_proj(w_in, mu):
    d = w_in.shape[0]
    w_t = jnp.swapaxes(w_in, 0, 1)
    zeros = lambda n: jnp.zeros((n, d), w_in.dtype)
    w = jnp.concatenate([w_t[:RWKV_COLS], zeros(COL_Q - RWKV_COLS),
                         w_t[RWKV_COLS:], zeros(IN_COLS_PACKED - COL_Q - ATT_COLS)], axis=0).astype(bf16)
    mu_lo = jnp.concatenate([mu[COL_LORA:RWKV_COLS], jnp.zeros((COL_Q - RWKV_COLS,), mu.dtype)])
    mus = (mu[0:1024][None], mu[1024:2048][None], mu[2048:3072][None], mu_lo[None])
    return w, mus


def _rope_freq():
    def inv_freq(head):
        half = head // 8
        return ROPE_THETA ** (-(lane % half).astype(f32) / half)
    lane = jnp.arange(LANES)
    att = inv_freq(ATT_HEAD)
    idx = inv_freq(IDX_HEAD)
    in_idx = (lane >= IDX_HEAD) & (lane < IDX_HEAD + IDX_HEAD // 4)
    return jnp.where(lane < ATT_HEAD // 4, att, jnp.where(in_idx, idx, 0.0)).astype(f32)[None]


def kernel(x, mem, positions, norm_mix, w_in, rwkv_mu, w_decay0, w_decay_up, a0, a_up, g_up, k_k, k_a, r_k,
           lnx_w, lnx_b, w_mix_out, norm_cross, norm_mem, w_q_cross, w_kv_cross, w_o_cross, norm_mlp,
           w_up, w_down, norm_final):
    batch, seq, d = x.shape
    n_mem = mem.shape[1]
    t = batch * seq
    depth = w_in.shape[0]
    tm = min(512, t)
    h = x.reshape(t, d)
    pos = positions.reshape(t, 1).astype(f32)
    mem2 = mem.reshape(batch * n_mem, d)
    row = lambda vct: vct.reshape(1, -1)

    for l in range(depth):
        w_in_p, mus = _pack_in_proj(w_in[l], rwkv_mu[l])
        zpad = jnp.zeros((DECAY_LORA, RWKV_WIDTH), f32)
        w_wa = jnp.concatenate([jnp.concatenate([w_decay_up[l], zpad], axis=1),
                                jnp.concatenate([zpad, a_up[l]], axis=1)], axis=0).astype(bf16)
        g_up_p = jnp.concatenate([g_up[l], jnp.zeros((LORA_BLOCK - LANES - GATE_LORA, RWKV_WIDTH), f32)],
                                 axis=0).astype(bf16)

        p = _norm_matmul(h, row(norm_mix[l]), w_in_p, f32, min(1024, t), 1536, w_is_transposed=True)
        y_rwkv = _rwkv(p, batch, seq, mus, row(w_decay0[l]), row(a0[l]), row(k_k[l]), row(k_a[l]),
                       row(lnx_w[l]), row(lnx_b[l]), row(r_k[l]), w_wa, g_up_p)
        q, qi, k_att, v_att, ki2, wi = _dsa_prep(p, pos, _rope_freq(), tm)
        y_att = _dsa(q, qi, wi, k_att, v_att, ki2, batch, seq, min(DSA_QUERY_BLOCK, seq))
        kv = _norm_matmul(mem2, row(norm_mem[l]), w_kv_cross[l].astype(bf16), bf16, min(512, batch * n_mem), 1024)
        h = _mix_cross(h, y_rwkv, y_att, w_mix_out[l].astype(bf16), row(norm_cross[l]), kv,
                       w_q_cross[l].astype(bf16), w_o_cross[l].astype(bf16), batch, seq, n_mem, tm)
        h = _mlp(h, row(norm_mlp[l]), w_up[l].astype(bf16), w_down[l].astype(bf16), row(norm_final),
                 l == depth - 1, tm, 2048)
    return h.reshape(batch, seq, d)
```

```python
import functools

import jax
import jax.numpy as jnp
from jax import lax
from jax.experimental import pallas as pl
from jax.experimental.pallas import tpu as pltpu

f32 = jnp.float32
bf16 = jnp.bfloat16

D_MODEL = 2048
RWKV_WIDTH = 1024
RWKV_HEAD = 64
DECAY_LORA = 64
AAA_LORA = 64
GATE_LORA = 160
RWKV_LN_EPS = 64e-5
ATT_WIDTH = 1024
ATT_HEAD = 128
ATT_HEADS = 8
ATT_KV_HEADS = 2
IDX_HEADS = 16
IDX_HEAD = 64
TOPK_MAX = 256
ROPE_THETA = 500000.0
CROSS_HEADS = 4
CROSS_HEAD = 128
D_FF = 4 * D_MODEL
NORM_EPS = 1e-5

LANES = 128
SUBLANES = 8
VMEM_LIMIT_BYTES = 56 * 1024 * 1024

COL_R, COL_K, COL_V, COL_LORA = 0, 1024, 2048, 3072
LORA_BLOCK = 384
RWKV_COLS, ATT_COLS = 3360, 2640
COL_Q = COL_LORA + LORA_BLOCK
COL_KATT = COL_Q + ATT_WIDTH
COL_VATT = COL_KATT + ATT_KV_HEADS * ATT_HEAD
COL_QI = COL_VATT + ATT_KV_HEADS * ATT_HEAD
COL_KIWI = COL_QI + IDX_HEADS * IDX_HEAD
IN_COLS_PACKED = COL_KIWI + LANES
RWKV_CHUNK = 64
_EXP_NEG_HALF = 0.6065306597126334
_LOG2_E = 1.4426950408889634
RWKV_BATCH_GROUP = 2
DSA_QUERY_BLOCK = 128
COUNT_CHAINS = 8
DSA_BUCKETS = 8


def _nt(a, b):
    return lax.dot_general(a, b, (((1,), (1,)), ((), ())), preferred_element_type=f32)


def _mm(a, b):
    return jnp.dot(a, b, preferred_element_type=f32)


def _rms(x, g):
    ms = jnp.mean(x * x, axis=-1, keepdims=True)
    return x * lax.rsqrt(ms + NORM_EPS) * g


def _params(*sem):
    return pltpu.CompilerParams(dimension_semantics=sem, vmem_limit_bytes=VMEM_LIMIT_BYTES)


def _norm_matmul_kernel(x_ref, g_ref, w_ref, o_ref, xn_ref, *, w_is_transposed):
    @pl.when(pl.program_id(1) == 0)
    def _():
        xn_ref[...] = _rms(x_ref[...], g_ref[...]).astype(bf16)

    dot = _nt if w_is_transposed else _mm
    o_ref[...] = dot(xn_ref[...], w_ref[...]).astype(o_ref.dtype)


def _norm_matmul(x, g, w, out_dtype, tm, tn, w_is_transposed=False):
    t, d = x.shape
    n = w.shape[0] if w_is_transposed else w.shape[1]
    w_spec = (pl.BlockSpec((tn, d), lambda i, j: (j, 0)) if w_is_transposed
              else pl.BlockSpec((d, tn), lambda i, j: (0, j)))
    return pl.pallas_call(
        functools.partial(_norm_matmul_kernel, w_is_transposed=w_is_transposed),
        grid=(t // tm, n // tn),
        in_specs=[
            pl.BlockSpec((tm, d), lambda i, j: (i, 0)),
            pl.BlockSpec((1, d), lambda i, j: (0, 0)),
            w_spec,
        ],
        out_specs=pl.BlockSpec((tm, tn), lambda i, j: (i, j)),
        out_shape=jax.ShapeDtypeStruct((t, n), out_dtype),
        scratch_shapes=[pltpu.VMEM((tm, d), bf16)],
        compiler_params=_params("parallel", "arbitrary"),
        name="norm_matmul",
    )(x, g, w)


def _seg_sum(x, first_half):
    s_a = jnp.sum(jnp.where(first_half, x, 0.0), axis=-1, keepdims=True)
    s_b = jnp.sum(jnp.where(first_half, 0.0, x), axis=-1, keepdims=True)
    return jnp.where(first_half, s_a, s_b)


def _rwkv_kernel(r_ref, k_ref, v_ref, lo_ref, mur_ref, muk_ref, muv_ref, mulo_ref,
                 w0_ref, a0_ref, kkw_ref, kaw_ref, lnw_ref, lnb_ref, rk_ref, wwa_ref, gup_ref,
                 y_ref, st_ref, pr_ref, pk_ref, pv_ref, plo_ref):
    G, C = r_ref.shape[0], r_ref.shape[1]
    n_pairs = RWKV_WIDTH // LANES

    @pl.when(pl.program_id(1) == 0)
    def _():
        st_ref[...] = jnp.zeros_like(st_ref)
        pr_ref[...] = jnp.zeros_like(pr_ref)
        pk_ref[...] = jnp.zeros_like(pk_ref)
        pv_ref[...] = jnp.zeros_like(pv_ref)
        plo_ref[...] = jnp.zeros_like(plo_ref)

    row = lax.broadcasted_iota(jnp.int32, (C, 1), 0)
    lane = lax.broadcasted_iota(jnp.int32, (1, LANES), 1)
    first_half = lane < RWKV_HEAD
    ti = lax.broadcasted_iota(jnp.int32, (C, C), 0)
    tj = lax.broadcasted_iota(jnp.int32, (C, C), 1)
    tri_b = jnp.where(ti >= tj, 1.0, 0.0).astype(bf16)
    t2 = lax.broadcasted_iota(jnp.int32, (C, 2 * C), 0)
    j2 = lax.broadcasted_iota(jnp.int32, (C, 2 * C), 1) % C
    tri2_strict = t2 > j2
    eye2 = jnp.where(t2 == j2, 1.0, 0.0)
    t4 = lax.broadcasted_iota(jnp.int32, (C, 4 * C), 0)
    j4 = lax.broadcasted_iota(jnp.int32, (C, 4 * C), 1) % C
    tri4_incl = t4 >= j4
    vi = lax.broadcasted_iota(jnp.int32, (LANES, LANES), 0)
    vj = lax.broadcasted_iota(jnp.int32, (LANES, LANES), 1)
    same_head = (vi < RWKV_HEAD) == (vj < RWKV_HEAD)

    def head_rows(x):
        xb = x.astype(bf16)
        zero = jnp.zeros((), bf16)
        return jnp.concatenate([jnp.where(first_half, xb, zero), jnp.where(first_half, zero, xb)], axis=0)

    def shifted(x_ref, prev_ref, mu_ref, g):
        x = x_ref[g]
        prev = jnp.where(row == 0, prev_ref[g], pltpu.roll(x, 1, 0))
        prev_ref[g] = x[C - 1:C, :]
        return x + (prev - x) * mu_ref[...]

    units = [(g, p) for g in range(G) for p in range(n_pairs)]
    lanes_of = lambda p: slice(p * LANES, (p + 1) * LANES)
    v_all, gate_all, rkk_all = [], [], []
    a_t, b_t, k_t, r_t, g_end, s0 = [], [], [], [], [], []
    for g in range(G):
        r = shifted(r_ref, pr_ref, mur_ref, g)
        k = shifted(k_ref, pk_ref, muk_ref, g)
        v = shifted(v_ref, pv_ref, muv_ref, g)
        lo = shifted(lo_ref, plo_ref, mulo_ref, g)
        wa_in = lo[:, :LANES]
        wa_in = jnp.where(first_half, 1.0 - 2.0 / (1.0 + jnp.exp(2.0 * wa_in)), wa_in).astype(bf16)
        wa = _mm(wa_in, wwa_ref[...])
        logdecay = -_EXP_NEG_HALF / (1.0 + jnp.exp(-(w0_ref[...] + wa[:, :RWKV_WIDTH])))
        a = 1.0 / (1.0 + jnp.exp(-(a0_ref[...] + wa[:, RWKV_WIDTH:])))
        gate_all.append(_mm((1.0 / (1.0 + jnp.exp(-lo[:, LANES:]))).astype(bf16), gup_ref[...]))
        ld_hi = logdecay.astype(bf16)
        ld_r1 = logdecay - ld_hi.astype(f32)
        ld_mid = ld_r1.astype(bf16)
        ld_lo = (ld_r1 - ld_mid.astype(f32)).astype(bf16)
        cum = _mm(tri_b, ld_hi) + _mm(tri_b, ld_mid) + _mm(tri_b, ld_lo)
        kk = k * kkw_ref[...]
        k2 = k * (1.0 + (a - 1.0) * kaw_ref[...])
        rkk_all.append(r * k2 * rk_ref[...])
        v_all.append(v)
        ecum = jnp.exp(cum)
        encum = jnp.exp(-cum)
        ecum_prev = jnp.exp(cum - logdecay)
        for p in range(n_pairs):
            sl = lanes_of(p)
            kkp = kk[:, sl]
            kkn = kkp * lax.rsqrt(jnp.maximum(_seg_sum(kkp * kkp, first_half), 1e-24))
            a_t.append(-kkn * ecum_prev[:, sl])
            b_t.append(kkn * a[:, sl] * encum[:, sl])
            k_t.append(k2[:, sl] * encum[:, sl])
            r_t.append(r[:, sl] * ecum[:, sl])
            g_end.append(ecum[C - 1:C, sl])
            s0.append(st_ref[g * n_pairs + p])

    n = len(units)
    vp = [v_all[g][:, lanes_of(p)] for g, p in units]
    v_rows = [head_rows(x) for x in vp]
    ar = [jnp.concatenate([a_t[i], r_t[i]], axis=0).astype(bf16) for i in range(n)]
    ar_all = [_nt(ar[i], jnp.concatenate([s0[i].astype(bf16), head_rows(b_t[i]), head_rows(k_t[i])], axis=0))
              for i in range(n)]
    ar0 = [x[:, :LANES] for x in ar_all]
    gram = [x[:, LANES:] for x in ar_all]
    l_b = [jnp.where(tri2_strict, x[:C, :2 * C], 0.0) for x in gram]
    l_k = [jnp.where(tri2_strict, x[:C, 2 * C:], 0.0).astype(bf16) for x in gram]
    m_bk = [jnp.where(tri4_incl, x[C:, :], 0.0).astype(bf16) for x in gram]
    w = [ar0[i][:C] + _mm(l_k[i], v_rows[i]) for i in range(n)]
    inv = [eye2 + x for x in l_b]
    l_pow_rows = [head_rows(x) for x in l_b]
    l_pow = [_mm(x.astype(bf16), y) for x, y in zip(l_b, l_pow_rows)]
    n_steps = C.bit_length() - 2
    for step in range(n_steps):
        l_pow_rows = [head_rows(x) for x in l_pow]
        if step < n_steps - 1:
            both = [_mm(jnp.concatenate([x, y], axis=0).astype(bf16), z)
                    for x, y, z in zip(inv, l_pow, l_pow_rows)]
            inv = [x + y[:C] for x, y in zip(inv, both)]
            l_pow = [y[C:] for y in both]
        else:
            inv = [x + _mm(x.astype(bf16), z) for x, z in zip(inv, l_pow_rows)]
    u = [_mm(inv[i].astype(bf16), head_rows(w[i])) for i in range(n)]
    y_all = [ar0[i][C:] + _mm(m_bk[i], jnp.concatenate([head_rows(u[i]), v_rows[i]], axis=0))
             for i in range(n)]
    upd = [_mm(jnp.concatenate([u[i], vp[i]], axis=0).T.astype(bf16),
               (jnp.concatenate([b_t[i], k_t[i]], axis=0) * g_end[i]).astype(bf16)) for i in range(n)]
    for i, (g, p) in enumerate(units):
        sl = lanes_of(p)
        st_ref[g * n_pairs + p] = s0[i] * g_end[i] + jnp.where(same_head, upd[i], 0.0)
        y = y_all[i]
        mean = _seg_sum(y, first_half) * (1.0 / RWKV_HEAD)
        yc = y - mean
        var = _seg_sum(yc * yc, first_half) * (1.0 / RWKV_HEAD)
        yn = yc * lax.rsqrt(var + RWKV_LN_EPS) * lnw_ref[:, sl] + lnb_ref[:, sl]
        bonus = _seg_sum(rkk_all[g][:, sl], first_half) * vp[i]
        y_ref[g, :, sl] = ((yn + bonus) * gate_all[g][:, sl]).astype(y_ref.dtype)


def _rwkv(p, batch, seq, mu, w0, a0, k_k, k_a, lnx_w, lnx_b, r_k, w_wa, g_up_p):
    C = RWKV_CHUNK
    G = RWKV_BATCH_GROUP if batch % RWKV_BATCH_GROUP == 0 else 1
    W = RWKV_WIDTH
    p3 = p.reshape(batch, seq, p.shape[-1])
    row_block = lambda width, col: pl.BlockSpec((G, C, width), lambda b, c: (b, c, col // width))
    vec = lambda width: pl.BlockSpec((1, width), lambda b, c: (0, 0))
    full = lambda arr: pl.BlockSpec(arr.shape, lambda b, c: (0, 0))
    mu_r, mu_k, mu_v, mu_lo = mu
    y = pl.pallas_call(
        _rwkv_kernel,
        grid=(batch // G, seq // C),
        in_specs=[row_block(W, COL_R), row_block(W, COL_K), row_block(W, COL_V),
                  row_block(LORA_BLOCK, COL_LORA),
                  vec(W), vec(W), vec(W), vec(LORA_BLOCK),
                  vec(W), vec(W), vec(W), vec(W), vec(W), vec(W), vec(W),
                  full(w_wa), full(g_up_p)],
        out_specs=pl.BlockSpec((G, C, W), lambda b, c: (b, c, 0)),
        out_shape=jax.ShapeDtypeStruct((batch, seq, W), bf16),
        scratch_shapes=[pltpu.VMEM((G * W // LANES, LANES, LANES), f32),
                        pltpu.VMEM((G, 1, W), f32), pltpu.VMEM((G, 1, W), f32), pltpu.VMEM((G, 1, W), f32),
                        pltpu.VMEM((G, 1, LORA_BLOCK), f32)],
        compiler_params=_params("parallel", "arbitrary"),
        name="rwkv7_chunked",
    )(p3, p3, p3, p3, mu_r, mu_k, mu_v, mu_lo, w0, a0, k_k, k_a, lnx_w, lnx_b, r_k, w_wa, g_up_p)
    return y.reshape(batch * seq, W)


def _rope(x, cos, sin_lo, sin_hi, half):
    n = x.shape[-1]
    return x * cos + pltpu.roll(x, n - half, 1) * sin_lo + pltpu.roll(x, half, 1) * sin_hi


def _dsa_prep_kernel(pos_ref, freq_ref, *refs):
    n_qi = IDX_HEADS * IDX_HEAD // LANES
    q_refs, refs = refs[:ATT_HEADS], refs[ATT_HEADS:]
    qi_refs, refs = refs[:n_qi], refs[n_qi:]
    ka_refs, refs = refs[:ATT_KV_HEADS], refs[ATT_KV_HEADS:]
    va_refs, refs = refs[:ATT_KV_HEADS], refs[ATT_KV_HEADS:]
    kiwi_ref, qo_ref, qio_ref, ko_ref, vo_ref, ki2_ref, wi_ref = refs
    pos = pos_ref[...]
    lane = lax.broadcasted_iota(jnp.int32, (1, LANES), 1)
    ang = pos * freq_ref[...]
    cos_t, sin_t = jnp.cos(ang), jnp.sin(ang)

    def tables(cos, sin, head, half):
        in_head = lane % head
        sin_lo = jnp.where(in_head < half, -sin, 0.0)
        sin_hi = jnp.where((in_head >= half) & (in_head < 2 * half), sin, 0.0)
        return cos, sin_lo, sin_hi

    att_lanes = lane < ATT_HEAD // 4
    upper = lane >= IDX_HEAD
    ca, sla, sha = tables(jnp.where(att_lanes, cos_t, 1.0), jnp.where(att_lanes, sin_t, 0.0),
                          ATT_HEAD, ATT_HEAD // 8)
    ci, sli, shi = tables(jnp.where(upper, cos_t, pltpu.roll(cos_t, IDX_HEAD, 1)),
                          jnp.where(upper, sin_t, pltpu.roll(sin_t, IDX_HEAD, 1)),
                          IDX_HEAD, IDX_HEAD // 8)
    lanes_of = lambda h: slice(h * LANES, (h + 1) * LANES)
    for h in range(ATT_HEADS):
        qo_ref[:, lanes_of(h)] = _rope(q_refs[h][...], ca, sla, sha, ATT_HEAD // 8).astype(bf16)
    for h in range(n_qi):
        qio_ref[:, lanes_of(h)] = _rope(qi_refs[h][...], ci, sli, shi, IDX_HEAD // 8).astype(bf16)
    for h in range(ATT_KV_HEADS):
        ko_ref[:, lanes_of(h)] = _rope(ka_refs[h][...], ca, sla, sha, ATT_HEAD // 8).astype(bf16)
        vo_ref[:, lanes_of(h)] = va_refs[h][...].astype(bf16)
    kiwi = kiwi_ref[...]
    ki_only = jnp.where(lane < IDX_HEAD, kiwi, 0.0)
    ki = _rope(ki_only, ci, sli, shi, IDX_HEAD // 8)
    ki2_ref[...] = (ki + pltpu.roll(ki, IDX_HEAD, 1)).astype(bf16)
    wi = pltpu.roll(kiwi, LANES - IDX_HEAD, 1)
    wi_ref[...] = jnp.where(lane < IDX_HEADS, wi, 0.0) * (IDX_HEADS ** -0.5 * IDX_HEAD ** -0.5)


def _dsa_prep(p, pos, freq, tm):
    t = p.shape[0]
    out = lambda width: pl.BlockSpec((tm, width), lambda i: (i, 0))
    vec = pl.BlockSpec((1, LANES), lambda i: (0, 0))
    kvw = ATT_KV_HEADS * ATT_HEAD
    head_cols = ([COL_Q + h * LANES for h in range(ATT_HEADS)]
                 + [COL_QI + h * LANES for h in range(IDX_HEADS * IDX_HEAD // LANES)]
                 + [COL_KATT + h * LANES for h in range(ATT_KV_HEADS)]
                 + [COL_VATT + h * LANES for h in range(ATT_KV_HEADS)]
                 + [COL_KIWI])
    head_blk = lambda col: pl.BlockSpec((tm, LANES), lambda i: (i, col // LANES))
    return pl.pallas_call(
        _dsa_prep_kernel,
        grid=(t // tm,),
        in_specs=[pl.BlockSpec((tm, 1), lambda i: (i, 0)), vec] + [head_blk(c) for c in head_cols],
        out_specs=[out(ATT_WIDTH), out(IDX_HEADS * IDX_HEAD), out(kvw), out(kvw), out(LANES), out(LANES)],
        out_shape=[jax.ShapeDtypeStruct((t, ATT_WIDTH), bf16),
                   jax.ShapeDtypeStruct((t, IDX_HEADS * IDX_HEAD), bf16),
                   jax.ShapeDtypeStruct((t, kvw), bf16),
                   jax.ShapeDtypeStruct((t, kvw), bf16),
                   jax.ShapeDtypeStruct((t, LANES), bf16),
                   jax.ShapeDtypeStruct((t, LANES), f32)],
        compiler_params=_params("parallel"),
        name="dsa_prep",
    )(pos, freq, *([p] * len(head_cols)))


def _dsa_kernel(q_ref, qi_ref, wi_ref, k_ref, v_ref, ki2_ref, o_ref, score_ref, key_ref, logit_ref,
                *, n_sel, bucket_len):
    tq = q_ref.shape[0]
    seq = k_ref.shape[0]
    bucket = ((pl.program_id(1) + 1) * tq - 1) // bucket_len
    for j in range(seq // bucket_len):
        pl.when(bucket == j)(functools.partial(
            _dsa_body, q_ref, qi_ref, wi_ref, k_ref, v_ref, ki2_ref, o_ref, score_ref, key_ref, logit_ref,
            n_sel=n_sel, s_len=(j + 1) * bucket_len))


def _dsa_body(q_ref, qi_ref, wi_ref, k_ref, v_ref, ki2_ref, o_ref, score_ref, key_ref, logit_ref,
              *, n_sel, s_len):
    tq = q_ref.shape[0]
    q0 = pl.program_id(1) * tq
    lane = lax.broadcasted_iota(jnp.int32, (1, LANES), 1)
    first_half = lane < IDX_HEAD
    ki2 = ki2_ref[0:s_len, :]
    wi = wi_ref[...]

    score = jnp.zeros((tq, s_len), f32)
    for h in range(IDX_HEADS):
        pair = h // 2
        m = first_half if h % 2 == 0 else jnp.logical_not(first_half)
        qm = jnp.where(m, qi_ref[:, pair * LANES:(pair + 1) * LANES], jnp.zeros((), bf16))
        rel = jnp.maximum(_nt(qm, ki2), 0.0)
        w_h = jnp.sum(jnp.where(lane == h, wi, 0.0), axis=-1, keepdims=True)
        score = score + rel * w_h

    group = ATT_HEADS // ATT_KV_HEADS
    for kv in range(ATT_KV_HEADS):
        qg = jnp.concatenate([q_ref[:, (kv * group + g) * LANES:(kv * group + g + 1) * LANES]
                              for g in range(group)], axis=0)
        logit_ref[kv, :, 0:s_len] = _nt(qg, k_ref[0:s_len, kv * LANES:(kv + 1) * LANES])

    q_pos = q0 + lax.broadcasted_iota(jnp.int32, (tq, 1), 0)
    k_pos = lax.broadcasted_iota(jnp.int32, (1, s_len), 1)
    causal = k_pos <= q_pos
    score = jnp.where(causal, score, -1e30) + 0.0
    score_ref[:, 0:s_len] = score

    n8 = s_len // SUBLANES
    bits = pltpu.bitcast(score.T, jnp.int32)
    key_t = jnp.where(bits >= 0, bits, bits ^ jnp.int32(0x7FFFFFFF))
    key_ref[0:n8] = key_t.reshape(n8, SUBLANES, tq)
    int_min = jnp.int32(-2 ** 31)

    def search(i, t_u):
        cand = t_u | lax.shift_right_logical(int_min, i)
        ge = key_ref[0:n8] >= (cand ^ int_min)[None]
        ones = jnp.where(ge, 1.0, 0.0).reshape(n8 // COUNT_CHAINS, COUNT_CHAINS, SUBLANES, tq)
        cnt = jnp.sum(jnp.sum(ones, axis=0), axis=0)
        for shift in (4, 2, 1):
            cnt = cnt + pltpu.roll(cnt, shift, 0)
        return jnp.where(cnt >= n_sel, cand, t_u)

    t_u = lax.fori_loop(0, 32, search, jnp.zeros((SUBLANES, tq), jnp.int32), unroll=4)
    t_key = t_u[0:1, :] ^ int_min
    thr_row = pltpu.bitcast(jnp.where(t_key >= 0, t_key, t_key ^ jnp.int32(0x7FFFFFFF)), f32)
    diag = lax.broadcasted_iota(jnp.int32, (tq, tq), 0) == lax.broadcasted_iota(jnp.int32, (tq, tq), 1)
    thr_col = jnp.sum(jnp.where(diag, thr_row, 0.0), axis=-1, keepdims=True)
    selected = (score_ref[:, 0:s_len] >= thr_col) & causal

    sel_g = jnp.concatenate([selected] * group, axis=0)
    probs, denoms = [], []
    for kv in range(ATT_KV_HEADS):
        s = jnp.where(sel_g, logit_ref[kv, :, 0:s_len], -jnp.inf)
        e = jnp.exp2((s - jnp.max(s, axis=-1, keepdims=True)) * (ATT_HEAD ** -0.5 * _LOG2_E))
        denoms.append(jnp.sum(e, axis=-1, keepdims=True))
        probs.append(e.astype(bf16))
    for kv in range(ATT_KV_HEADS):
        o = _mm(probs[kv], v_ref[0:s_len, kv * LANES:(kv + 1) * LANES]) / denoms[kv]
        for g in range(group):
            hq = kv * group + g
            o_ref[:, hq * LANES:(hq + 1) * LANES] = o[g * tq:(g + 1) * tq].astype(o_ref.dtype)


def _dsa(q, qi, wi, k, v, ki2, batch, seq, tq):
    nq = seq // tq
    n_sel = min(TOPK_MAX, seq // 4)
    qblk = lambda width: pl.BlockSpec((tq, width), lambda b, i: (b * nq + i, 0))
    kblk = lambda width: pl.BlockSpec((seq, width), lambda b, i: (b, 0))
    kvw = ATT_KV_HEADS * ATT_HEAD
    bucket_len = max(tq, seq // DSA_BUCKETS)
    return pl.pallas_call(
        functools.partial(_dsa_kernel, n_sel=n_sel, bucket_len=bucket_len),
        grid=(batch, nq),
        in_specs=[qblk(ATT_WIDTH), qblk(IDX_HEADS * IDX_HEAD), qblk(LANES),
                  kblk(kvw), kblk(kvw), kblk(LANES)],
        out_specs=qblk(ATT_WIDTH),
        out_shape=jax.ShapeDtypeStruct((batch * seq, ATT_WIDTH), bf16),
        scratch_shapes=[pltpu.VMEM((tq, seq), f32),
                        pltpu.VMEM((seq // SUBLANES, SUBLANES, tq), jnp.int32),
                        pltpu.VMEM((ATT_KV_HEADS, (ATT_HEADS // ATT_KV_HEADS) * tq, seq), f32)],
        compiler_params=_params("parallel", "arbitrary"),
        name="dsa_attention",
    )(q, qi, wi, k, v, ki2)


def _mix_cross_kernel(x_ref, ya_ref, yb_ref, wa_ref, wb_ref, g_ref, kv_ref, wq_ref, wo_ref, o_ref):
    h = x_ref[...] + _mm(ya_ref[...], wa_ref[...]) + _mm(yb_ref[...], wb_ref[...])
    hn = _rms(h, g_ref[...]).astype(bf16)
    q = _mm(hn, wq_ref[...]).astype(bf16)
    width = CROSS_HEADS * CROSS_HEAD
    cols = lambda hd, base=0: slice(base + hd * CROSS_HEAD, base + (hd + 1) * CROSS_HEAD)
    logits = [_nt(q[:, cols(hd)], kv_ref[:, cols(hd)]) * (CROSS_HEAD ** -0.5) for hd in range(CROSS_HEADS)]
    exps = [jnp.exp(s - jnp.max(s, axis=-1, keepdims=True)) for s in logits]
    outs = [(_mm(e.astype(bf16), kv_ref[:, cols(hd, width)]) / jnp.sum(e, axis=-1, keepdims=True)).astype(bf16)
            for hd, e in enumerate(exps)]
    o = jnp.concatenate(outs, axis=-1)
    o_ref[...] = h + _mm(o, wo_ref[...])


def _mix_cross(x, ya, yb, w_mix, g, kv, wq, wo, batch, seq, n_mem, tm):
    d = x.shape[1]
    ka, kb = ya.shape[1], yb.shape[1]
    ns = seq // tm
    width = CROSS_HEADS * CROSS_HEAD
    rows = lambda cols: pl.BlockSpec((tm, cols), lambda b, i: (b * ns + i, 0))
    const = lambda shape, idx: pl.BlockSpec(shape, lambda b, i: idx, pipeline_mode=pl.Buffered(1))
    return pl.pallas_call(
        _mix_cross_kernel,
        grid=(batch, ns),
        in_specs=[rows(d), rows(ka), rows(kb),
                  const((ka, d), (0, 0)), const((kb, d), (ka // kb, 0)),
                  pl.BlockSpec((1, d), lambda b, i: (0, 0)),
                  pl.BlockSpec((n_mem, 2 * width), lambda b, i: (b, 0)),
                  const((d, width), (0, 0)), const((width, d), (0, 0))],
        out_specs=rows(d),
        out_shape=jax.ShapeDtypeStruct(x.shape, f32),
        compiler_params=_params("parallel", "arbitrary"),
        name="mix_out_cross_attention",
    )(x, ya, yb, w_mix, w_mix, g, kv, wq, wo)


def _mlp_kernel(h_ref, g_ref, wu_ref, wd_ref, gf_ref, o_ref, hn_ref, *, final_norm):
    j = pl.program_id(1)

    @pl.when(j == 0)
    def _():
        h = h_ref[...]
        hn_ref[...] = _rms(h, g_ref[...]).astype(bf16)
        o_ref[...] = h

    u = jnp.maximum(_mm(hn_ref[...], wu_ref[...]), 0.0)
    o_ref[...] += _mm((u * u).astype(bf16), wd_ref[...])

    if final_norm:
        @pl.when(j == pl.num_programs(1) - 1)
        def _():
            o_ref[...] = _rms(o_ref[...], gf_ref[...])


def _mlp(h, g, wu, wd, gf, final_norm, tm, tf):
    t, d = h.shape
    dff = wu.shape[1]
    return pl.pallas_call(
        functools.partial(_mlp_kernel, final_norm=final_norm),
        grid=(t // tm, dff // tf),
        in_specs=[pl.BlockSpec((tm, d), lambda i, j: (i, 0)),
                  pl.BlockSpec((1, d), lambda i, j: (0, 0)),
                  pl.BlockSpec((d, tf), lambda i, j: (0, j)),
                  pl.BlockSpec((tf, d), lambda i, j: (j, 0)),
                  pl.BlockSpec((1, d), lambda i, j: (0, 0))],
        out_specs=pl.BlockSpec((tm, d), lambda i, j: (i, 0)),
        out_shape=jax.ShapeDtypeStruct((t, d), f32),
        scratch_shapes=[pltpu.VMEM((tm, d), bf16)],
        compiler_params=_params("parallel", "arbitrary"),
        name="mlp_final_norm",
    )(h, g, wu, wd, gf)


def _pack_in_proj(w_in, mu):
    d = w_in.shape[0]
    w_t = jnp.swapaxes(w_in, 0, 1)
    zeros = lambda n: jnp.zeros((n, d), w_in.dtype)
    w = jnp.concatenate([w_t[:RWKV_COLS], zeros(COL_Q - RWKV_COLS),
                         w_t[RWKV_COLS:], zeros(IN_COLS_PACKED - COL_Q - ATT_COLS)], axis=0).astype(bf16)
    mu_lo = jnp.concatenate([mu[COL_LORA:RWKV_COLS], jnp.zeros((COL_Q - RWKV_COLS,), mu.dtype)])
    mus = (mu[0:1024][None], mu[1024:2048][None], mu[2048:3072][None], mu_lo[None])
    return w, mus


def _rope_freq():
    def inv_freq(head):
        half = head // 8
        return ROPE_THETA ** (-(lane % half).astype(f32) / half)
    lane = jnp.arange(LANES)
    att = inv_freq(ATT_HEAD)
    idx = inv_freq(IDX_HEAD)
    in_idx = (lane >= IDX_HEAD) & (lane < IDX_HEAD + IDX_HEAD // 4)
    return jnp.where(lane < ATT_HEAD // 4, att, jnp.where(in_idx, idx, 0.0)).astype(f32)[None]


def kernel(x, mem, positions, norm_mix, w_in, rwkv_mu, w_decay0, w_decay_up, a0, a_up, g_up, k_k, k_a, r_k,
           lnx_w, lnx_b, w_mix_out, norm_cross, norm_mem, w_q_cross, w_kv_cross, w_o_cross, norm_mlp,
           w_up, w_down, norm_final):
    batch, seq, d = x.shape
    n_mem = mem.shape[1]
    t = batch * seq
    depth = w_in.shape[0]
    tm = min(512, t)
    h = x.reshape(t, d)
    pos = positions.reshape(t, 1).astype(f32)
    mem2 = mem.reshape(batch * n_mem, d)
    row = lambda vct: vct.reshape(1, -1)

    for l in range(depth):
        w_in_p, mus = _pack_in_proj(w_in[l], rwkv_mu[l])
        zpad = jnp.zeros((DECAY_LORA, RWKV_WIDTH), f32)
        w_wa = jnp.concatenate([jnp.concatenate([w_decay_up[l], zpad], axis=1),
                                jnp.concatenate([zpad, a_up[l]], axis=1)], axis=0).astype(bf16)
        g_up_p = jnp.concatenate([g_up[l], jnp.zeros((LORA_BLOCK - LANES - GATE_LORA, RWKV_WIDTH), f32)],
                                 axis=0).astype(bf16)

        p = _norm_matmul(h, row(norm_mix[l]), w_in_p, f32, min(1024, t), 2048, w_is_transposed=True)
        y_rwkv = _rwkv(p, batch, seq, mus, row(w_decay0[l]), row(a0[l]), row(k_k[l]), row(k_a[l]),
                       row(lnx_w[l]), row(lnx_b[l]), row(r_k[l]), w_wa, g_up_p)
        q, qi, k_att, v_att, ki2, wi = _dsa_prep(p, pos, _rope_freq(), tm)
        y_att = _dsa(q, qi, wi, k_att, v_att, ki2, batch, seq, min(DSA_QUERY_BLOCK, seq))
        kv = _norm_matmul(mem2, row(norm_mem[l]), w_kv_cross[l].astype(bf16), bf16, min(512, batch * n_mem), 1024)
        h = _mix_cross(h, y_rwkv, y_att, w_mix_out[l].astype(bf16), row(norm_cross[l]), kv,
                       w_q_cross[l].astype(bf16), w_o_cross[l].astype(bf16), batch, seq, n_mem, tm)
        h = _mlp(h, row(norm_mlp[l]), w_up[l].astype(bf16), w_down[l].astype(bf16), row(norm_final),
                 l == depth - 1, tm, 2048)
    return h.reshape(batch, seq, d)
```

```python
import functools

import jax
import jax.numpy as jnp
from jax import lax
from jax.experimental import pallas as pl
from jax.experimental.pallas import tpu as pltpu

f32 = jnp.float32
bf16 = jnp.bfloat16

D_MODEL = 2048
RWKV_WIDTH = 1024
RWKV_HEAD = 64
DECAY_LORA = 64
AAA_LORA = 64
GATE_LORA = 160
RWKV_LN_EPS = 64e-5
ATT_WIDTH = 1024
ATT_HEAD = 128
ATT_HEADS = 8
ATT_KV_HEADS = 2
IDX_HEADS = 16
IDX_HEAD = 64
TOPK_MAX = 256
ROPE_THETA = 500000.0
CROSS_HEADS = 4
CROSS_HEAD = 128
D_FF = 4 * D_MODEL
NORM_EPS = 1e-5

LANES = 128
SUBLANES = 8
VMEM_LIMIT_BYTES = 56 * 1024 * 1024

COL_R, COL_K, COL_V, COL_LORA = 0, 1024, 2048, 3072
LORA_BLOCK = 384
RWKV_COLS, ATT_COLS = 3360, 2640
COL_Q = COL_LORA + LORA_BLOCK
COL_KATT = COL_Q + ATT_WIDTH
COL_VATT = COL_KATT + ATT_KV_HEADS * ATT_HEAD
COL_QI = COL_VATT + ATT_KV_HEADS * ATT_HEAD
COL_KIWI = COL_QI + IDX_HEADS * IDX_HEAD
IN_COLS_PACKED = COL_KIWI + LANES
RWKV_CHUNK = 64
_EXP_NEG_HALF = 0.6065306597126334
_LOG2_E = 1.4426950408889634
RWKV_BATCH_GROUP = 2
DSA_QUERY_BLOCK = 128
COUNT_CHAINS = 16
DSA_BUCKETS = 8


def _nt(a, b):
    return lax.dot_general(a, b, (((1,), (1,)), ((), ())), preferred_element_type=f32)


def _mm(a, b):
    return jnp.dot(a, b, preferred_element_type=f32)


def _rms(x, g):
    ms = jnp.mean(x * x, axis=-1, keepdims=True)
    return x * lax.rsqrt(ms + NORM_EPS) * g


def _params(*sem):
    return pltpu.CompilerParams(dimension_semantics=sem, vmem_limit_bytes=VMEM_LIMIT_BYTES)


def _norm_matmul_kernel(x_ref, g_ref, w_ref, o_ref, xn_ref, *, w_is_transposed):
    @pl.when(pl.program_id(1) == 0)
    def _():
        xn_ref[...] = _rms(x_ref[...], g_ref[...]).astype(bf16)

    dot = _nt if w_is_transposed else _mm
    o_ref[...] = dot(xn_ref[...], w_ref[...]).astype(o_ref.dtype)


def _norm_matmul(x, g, w, out_dtype, tm, tn, w_is_transposed=False):
    t, d = x.shape
    n = w.shape[0] if w_is_transposed else w.shape[1]
    w_spec = (pl.BlockSpec((tn, d), lambda i, j: (j, 0)) if w_is_transposed
              else pl.BlockSpec((d, tn), lambda i, j: (0, j)))
    return pl.pallas_call(
        functools.partial(_norm_matmul_kernel, w_is_transposed=w_is_transposed),
        grid=(t // tm, n // tn),
        in_specs=[
            pl.BlockSpec((tm, d), lambda i, j: (i, 0)),
            pl.BlockSpec((1, d), lambda i, j: (0, 0)),
            w_spec,
        ],
        out_specs=pl.BlockSpec((tm, tn), lambda i, j: (i, j)),
        out_shape=jax.ShapeDtypeStruct((t, n), out_dtype),
        scratch_shapes=[pltpu.VMEM((tm, d), bf16)],
        compiler_params=_params("parallel", "arbitrary"),
        name="norm_matmul",
    )(x, g, w)


def _seg_sum(x, first_half):
    s_a = jnp.sum(jnp.where(first_half, x, 0.0), axis=-1, keepdims=True)
    s_b = jnp.sum(jnp.where(first_half, 0.0, x), axis=-1, keepdims=True)
    return jnp.where(first_half, s_a, s_b)


def _rwkv_kernel(r_ref, k_ref, v_ref, lo_ref, mur_ref, muk_ref, muv_ref, mulo_ref,
                 w0_ref, a0_ref, kkw_ref, kaw_ref, lnw_ref, lnb_ref, rk_ref, wwa_ref, gup_ref,
                 y_ref, st_ref, pr_ref, pk_ref, pv_ref, plo_ref):
    G, C = r_ref.shape[0], r_ref.shape[1]
    n_pairs = RWKV_WIDTH // LANES

    @pl.when(pl.program_id(1) == 0)
    def _():
        st_ref[...] = jnp.zeros_like(st_ref)
        pr_ref[...] = jnp.zeros_like(pr_ref)
        pk_ref[...] = jnp.zeros_like(pk_ref)
        pv_ref[...] = jnp.zeros_like(pv_ref)
        plo_ref[...] = jnp.zeros_like(plo_ref)

    row = lax.broadcasted_iota(jnp.int32, (C, 1), 0)
    lane = lax.broadcasted_iota(jnp.int32, (1, LANES), 1)
    first_half = lane < RWKV_HEAD
    ti = lax.broadcasted_iota(jnp.int32, (C, C), 0)
    tj = lax.broadcasted_iota(jnp.int32, (C, C), 1)
    tri_b = jnp.where(ti >= tj, 1.0, 0.0).astype(bf16)
    t2 = lax.broadcasted_iota(jnp.int32, (C, 2 * C), 0)
    j2 = lax.broadcasted_iota(jnp.int32, (C, 2 * C), 1) % C
    tri2_strict = t2 > j2
    eye2 = jnp.where(t2 == j2, 1.0, 0.0)
    t4 = lax.broadcasted_iota(jnp.int32, (C, 4 * C), 0)
    j4 = lax.broadcasted_iota(jnp.int32, (C, 4 * C), 1) % C
    tri4_incl = t4 >= j4
    vi = lax.broadcasted_iota(jnp.int32, (LANES, LANES), 0)
    vj = lax.broadcasted_iota(jnp.int32, (LANES, LANES), 1)
    same_head = (vi < RWKV_HEAD) == (vj < RWKV_HEAD)

    def head_rows(x):
        xb = x.astype(bf16)
        zero = jnp.zeros((), bf16)
        return jnp.concatenate([jnp.where(first_half, xb, zero), jnp.where(first_half, zero, xb)], axis=0)

    def shifted(x_ref, prev_ref, mu_ref, g):
        x = x_ref[g]
        prev = jnp.where(row == 0, prev_ref[g], pltpu.roll(x, 1, 0))
        prev_ref[g] = x[C - 1:C, :]
        return x + (prev - x) * mu_ref[...]

    units = [(g, p) for g in range(G) for p in range(n_pairs)]
    lanes_of = lambda p: slice(p * LANES, (p + 1) * LANES)
    v_all, gate_all, rkk_all = [], [], []
    a_t, b_t, k_t, r_t, g_end, s0 = [], [], [], [], [], []
    for g in range(G):
        r = shifted(r_ref, pr_ref, mur_ref, g)
        k = shifted(k_ref, pk_ref, muk_ref, g)
        v = shifted(v_ref, pv_ref, muv_ref, g)
        lo = shifted(lo_ref, plo_ref, mulo_ref, g)
        wa_in = lo[:, :LANES]
        wa_in = jnp.where(first_half, 1.0 - 2.0 / (1.0 + jnp.exp(2.0 * wa_in)), wa_in).astype(bf16)
        wa = _mm(wa_in, wwa_ref[...])
        logdecay = -_EXP_NEG_HALF / (1.0 + jnp.exp(-(w0_ref[...] + wa[:, :RWKV_WIDTH])))
        a = 1.0 / (1.0 + jnp.exp(-(a0_ref[...] + wa[:, RWKV_WIDTH:])))
        gate_all.append(_mm((1.0 / (1.0 + jnp.exp(-lo[:, LANES:]))).astype(bf16), gup_ref[...]))
        ld_hi = logdecay.astype(bf16)
        ld_r1 = logdecay - ld_hi.astype(f32)
        ld_mid = ld_r1.astype(bf16)
        ld_lo = (ld_r1 - ld_mid.astype(f32)).astype(bf16)
        cum = _mm(tri_b, ld_hi) + _mm(tri_b, ld_mid) + _mm(tri_b, ld_lo)
        kk = k * kkw_ref[...]
        k2 = k * (1.0 + (a - 1.0) * kaw_ref[...])
        rkk_all.append(r * k2 * rk_ref[...])
        v_all.append(v)
        ecum = jnp.exp(cum)
        encum = jnp.exp(-cum)
        ecum_prev = jnp.exp(cum - logdecay)
        for p in range(n_pairs):
            sl = lanes_of(p)
            kkp = kk[:, sl]
            kkn = kkp * lax.rsqrt(jnp.maximum(_seg_sum(kkp * kkp, first_half), 1e-24))
            a_t.append(-kkn * ecum_prev[:, sl])
            b_t.append(kkn * a[:, sl] * encum[:, sl])
            k_t.append(k2[:, sl] * encum[:, sl])
            r_t.append(r[:, sl] * ecum[:, sl])
            g_end.append(ecum[C - 1:C, sl])
            s0.append(st_ref[g * n_pairs + p])

    n = len(units)
    vp = [v_all[g][:, lanes_of(p)] for g, p in units]
    v_rows = [head_rows(x) for x in vp]
    ar = [jnp.concatenate([a_t[i], r_t[i]], axis=0).astype(bf16) for i in range(n)]
    ar_all = [_nt(ar[i], jnp.concatenate([s0[i].astype(bf16), head_rows(b_t[i]), head_rows(k_t[i])], axis=0))
              for i in range(n)]
    ar0 = [x[:, :LANES] for x in ar_all]
    gram = [x[:, LANES:] for x in ar_all]
    l_b = [jnp.where(tri2_strict, x[:C, :2 * C], 0.0) for x in gram]
    l_k = [jnp.where(tri2_strict, x[:C, 2 * C:], 0.0).astype(bf16) for x in gram]
    m_bk = [jnp.where(tri4_incl, x[C:, :], 0.0).astype(bf16) for x in gram]
    w = [ar0[i][:C] + _mm(l_k[i], v_rows[i]) for i in range(n)]
    inv = [eye2 + x for x in l_b]
    l_pow_rows = [head_rows(x) for x in l_b]
    l_pow = [_mm(x.astype(bf16), y) for x, y in zip(l_b, l_pow_rows)]
    n_steps = C.bit_length() - 2
    for step in range(n_steps):
        l_pow_rows = [head_rows(x) for x in l_pow]
        if step < n_steps - 1:
            both = [_mm(jnp.concatenate([x, y], axis=0).astype(bf16), z)
                    for x, y, z in zip(inv, l_pow, l_pow_rows)]
            inv = [x + y[:C] for x, y in zip(inv, both)]
            l_pow = [y[C:] for y in both]
        else:
            inv = [x + _mm(x.astype(bf16), z) for x, z in zip(inv, l_pow_rows)]
    u = [_mm(inv[i].astype(bf16), head_rows(w[i])) for i in range(n)]
    y_all = [ar0[i][C:] + _mm(m_bk[i], jnp.concatenate([head_rows(u[i]), v_rows[i]], axis=0))
             for i in range(n)]
    upd = [_mm(jnp.concatenate([u[i], vp[i]], axis=0).T.astype(bf16),
               (jnp.concatenate([b_t[i], k_t[i]], axis=0) * g_end[i]).astype(bf16)) for i in range(n)]
    for i, (g, p) in enumerate(units):
        sl = lanes_of(p)
        st_ref[g * n_pairs + p] = s0[i] * g_end[i] + jnp.where(same_head, upd[i], 0.0)
        y = y_all[i]
        mean = _seg_sum(y, first_half) * (1.0 / RWKV_HEAD)
        yc = y - mean
        var = _seg_sum(yc * yc, first_half) * (1.0 / RWKV_HEAD)
        yn = yc * lax.rsqrt(var + RWKV_LN_EPS) * lnw_ref[:, sl] + lnb_ref[:, sl]
        bonus = _seg_sum(rkk_all[g][:, sl], first_half) * vp[i]
        y_ref[g, :, sl] = ((yn + bonus) * gate_all[g][:, sl]).astype(y_ref.dtype)


def _rwkv(p, batch, seq, mu, w0, a0, k_k, k_a, lnx_w, lnx_b, r_k, w_wa, g_up_p):
    C = RWKV_CHUNK
    G = RWKV_BATCH_GROUP if batch % RWKV_BATCH_GROUP == 0 else 1
    W = RWKV_WIDTH
    p3 = p.reshape(batch, seq, p.shape[-1])
    row_block = lambda width, col: pl.BlockSpec((G, C, width), lambda b, c: (b, c, col // width))
    vec = lambda width: pl.BlockSpec((1, width), lambda b, c: (0, 0))
    full = lambda arr: pl.BlockSpec(arr.shape, lambda b, c: (0, 0))
    mu_r, mu_k, mu_v, mu_lo = mu
    y = pl.pallas_call(
        _rwkv_kernel,
        grid=(batch // G, seq // C),
        in_specs=[row_block(W, COL_R), row_block(W, COL_K), row_block(W, COL_V),
                  row_block(LORA_BLOCK, COL_LORA),
                  vec(W), vec(W), vec(W), vec(LORA_BLOCK),
                  vec(W), vec(W), vec(W), vec(W), vec(W), vec(W), vec(W),
                  full(w_wa), full(g_up_p)],
        out_specs=pl.BlockSpec((G, C, W), lambda b, c: (b, c, 0)),
        out_shape=jax.ShapeDtypeStruct((batch, seq, W), bf16),
        scratch_shapes=[pltpu.VMEM((G * W // LANES, LANES, LANES), f32),
                        pltpu.VMEM((G, 1, W), f32), pltpu.VMEM((G, 1, W), f32), pltpu.VMEM((G, 1, W), f32),
                        pltpu.VMEM((G, 1, LORA_BLOCK), f32)],
        compiler_params=_params("parallel", "arbitrary"),
        name="rwkv7_chunked",
    )(p3, p3, p3, p3, mu_r, mu_k, mu_v, mu_lo, w0, a0, k_k, k_a, lnx_w, lnx_b, r_k, w_wa, g_up_p)
    return y.reshape(batch * seq, W)


def _rope(x, cos, sin_lo, sin_hi, half):
    n = x.shape[-1]
    return x * cos + pltpu.roll(x, n - half, 1) * sin_lo + pltpu.roll(x, half, 1) * sin_hi


def _dsa_prep_kernel(pos_ref, freq_ref, *refs):
    n_qi = IDX_HEADS * IDX_HEAD // LANES
    q_refs, refs = refs[:ATT_HEADS], refs[ATT_HEADS:]
    qi_refs, refs = refs[:n_qi], refs[n_qi:]
    ka_refs, refs = refs[:ATT_KV_HEADS], refs[ATT_KV_HEADS:]
    va_refs, refs = refs[:ATT_KV_HEADS], refs[ATT_KV_HEADS:]
    kiwi_ref, qo_ref, qio_ref, ko_ref, vo_ref, ki2_ref, wi_ref = refs
    pos = pos_ref[...]
    lane = lax.broadcasted_iota(jnp.int32, (1, LANES), 1)
    ang = pos * freq_ref[...]
    cos_t, sin_t = jnp.cos(ang), jnp.sin(ang)

    def tables(cos, sin, head, half):
        in_head = lane % head
        sin_lo = jnp.where(in_head < half, -sin, 0.0)
        sin_hi = jnp.where((in_head >= half) & (in_head < 2 * half), sin, 0.0)
        return cos, sin_lo, sin_hi

    att_lanes = lane < ATT_HEAD // 4
    upper = lane >= IDX_HEAD
    ca, sla, sha = tables(jnp.where(att_lanes, cos_t, 1.0), jnp.where(att_lanes, sin_t, 0.0),
                          ATT_HEAD, ATT_HEAD // 8)
    ci, sli, shi = tables(jnp.where(upper, cos_t, pltpu.roll(cos_t, IDX_HEAD, 1)),
                          jnp.where(upper, sin_t, pltpu.roll(sin_t, IDX_HEAD, 1)),
                          IDX_HEAD, IDX_HEAD // 8)
    lanes_of = lambda h: slice(h * LANES, (h + 1) * LANES)
    for h in range(ATT_HEADS):
        qo_ref[:, lanes_of(h)] = _rope(q_refs[h][...], ca, sla, sha, ATT_HEAD // 8).astype(bf16)
    for h in range(n_qi):
        qio_ref[:, lanes_of(h)] = _rope(qi_refs[h][...], ci, sli, shi, IDX_HEAD // 8).astype(bf16)
    for h in range(ATT_KV_HEADS):
        ko_ref[:, lanes_of(h)] = _rope(ka_refs[h][...], ca, sla, sha, ATT_HEAD // 8).astype(bf16)
        vo_ref[:, lanes_of(h)] = va_refs[h][...].astype(bf16)
    kiwi = kiwi_ref[...]
    ki_only = jnp.where(lane < IDX_HEAD, kiwi, 0.0)
    ki = _rope(ki_only, ci, sli, shi, IDX_HEAD // 8)
    ki2_ref[...] = (ki + pltpu.roll(ki, IDX_HEAD, 1)).astype(bf16)
    wi = pltpu.roll(kiwi, LANES - IDX_HEAD, 1)
    wi_ref[...] = jnp.where(lane < IDX_HEADS, wi, 0.0) * (IDX_HEADS ** -0.5 * IDX_HEAD ** -0.5)


def _dsa_prep(p, pos, freq, tm):
    t = p.shape[0]
    out = lambda width: pl.BlockSpec((tm, width), lambda i: (i, 0))
    vec = pl.BlockSpec((1, LANES), lambda i: (0, 0))
    kvw = ATT_KV_HEADS * ATT_HEAD
    head_cols = ([COL_Q + h * LANES for h in range(ATT_HEADS)]
                 + [COL_QI + h * LANES for h in range(IDX_HEADS * IDX_HEAD // LANES)]
                 + [COL_KATT + h * LANES for h in range(ATT_KV_HEADS)]
                 + [COL_VATT + h * LANES for h in range(ATT_KV_HEADS)]
                 + [COL_KIWI])
    head_blk = lambda col: pl.BlockSpec((tm, LANES), lambda i: (i, col // LANES))
    return pl.pallas_call(
        _dsa_prep_kernel,
        grid=(t // tm,),
        in_specs=[pl.BlockSpec((tm, 1), lambda i: (i, 0)), vec] + [head_blk(c) for c in head_cols],
        out_specs=[out(ATT_WIDTH), out(IDX_HEADS * IDX_HEAD), out(kvw), out(kvw), out(LANES), out(LANES)],
        out_shape=[jax.ShapeDtypeStruct((t, ATT_WIDTH), bf16),
                   jax.ShapeDtypeStruct((t, IDX_HEADS * IDX_HEAD), bf16),
                   jax.ShapeDtypeStruct((t, kvw), bf16),
                   jax.ShapeDtypeStruct((t, kvw), bf16),
                   jax.ShapeDtypeStruct((t, LANES), bf16),
                   jax.ShapeDtypeStruct((t, LANES), f32)],
        compiler_params=_params("parallel"),
        name="dsa_prep",
    )(pos, freq, *([p] * len(head_cols)))


def _dsa_kernel(q_ref, qi_ref, wi_ref, k_ref, v_ref, ki2_ref, o_ref, score_ref, key_ref, logit_ref,
                *, n_sel, bucket_len):
    tq = q_ref.shape[0]
    seq = k_ref.shape[0]
    bucket = ((pl.program_id(1) + 1) * tq - 1) // bucket_len
    for j in range(seq // bucket_len):
        pl.when(bucket == j)(functools.partial(
            _dsa_body, q_ref, qi_ref, wi_ref, k_ref, v_ref, ki2_ref, o_ref, score_ref, key_ref, logit_ref,
            n_sel=n_sel, s_len=(j + 1) * bucket_len))


def _dsa_body(q_ref, qi_ref, wi_ref, k_ref, v_ref, ki2_ref, o_ref, score_ref, key_ref, logit_ref,
              *, n_sel, s_len):
    tq = q_ref.shape[0]
    q0 = pl.program_id(1) * tq
    lane = lax.broadcasted_iota(jnp.int32, (1, LANES), 1)
    first_half = lane < IDX_HEAD
    ki2 = ki2_ref[0:s_len, :]
    wi = wi_ref[...]

    score = jnp.zeros((tq, s_len), f32)
    for h in range(IDX_HEADS):
        pair = h // 2
        m = first_half if h % 2 == 0 else jnp.logical_not(first_half)
        qm = jnp.where(m, qi_ref[:, pair * LANES:(pair + 1) * LANES], jnp.zeros((), bf16))
        rel = jnp.maximum(_nt(qm, ki2), 0.0)
        w_h = jnp.sum(jnp.where(lane == h, wi, 0.0), axis=-1, keepdims=True)
        score = score + rel * w_h

    group = ATT_HEADS // ATT_KV_HEADS
    for kv in range(ATT_KV_HEADS):
        qg = jnp.concatenate([q_ref[:, (kv * group + g) * LANES:(kv * group + g + 1) * LANES]
                              for g in range(group)], axis=0)
        logit_ref[kv, :, 0:s_len] = _nt(qg, k_ref[0:s_len, kv * LANES:(kv + 1) * LANES])

    q_pos = q0 + lax.broadcasted_iota(jnp.int32, (tq, 1), 0)
    k_pos = lax.broadcasted_iota(jnp.int32, (1, s_len), 1)
    causal = k_pos <= q_pos
    score = jnp.where(causal, score, -1e30) + 0.0
    score_ref[:, 0:s_len] = score

    n8 = s_len // SUBLANES
    bits = pltpu.bitcast(score.T, jnp.int32)
    key_t = jnp.where(bits >= 0, bits, bits ^ jnp.int32(0x7FFFFFFF))
    key_ref[0:n8] = key_t.reshape(n8, SUBLANES, tq)
    int_min = jnp.int32(-2 ** 31)

    def search(i, t_u):
        cand = t_u | lax.shift_right_logical(int_min, i)
        ge = key_ref[0:n8] >= (cand ^ int_min)[None]
        ones = jnp.where(ge, 1.0, 0.0).reshape(n8 // COUNT_CHAINS, COUNT_CHAINS, SUBLANES, tq)
        cnt = jnp.sum(jnp.sum(ones, axis=0), axis=0)
        for shift in (4, 2, 1):
            cnt = cnt + pltpu.roll(cnt, shift, 0)
        return jnp.where(cnt >= n_sel, cand, t_u)

    t_u = lax.fori_loop(0, 32, search, jnp.zeros((SUBLANES, tq), jnp.int32), unroll=4)
    t_key = t_u[0:1, :] ^ int_min
    thr_row = pltpu.bitcast(jnp.where(t_key >= 0, t_key, t_key ^ jnp.int32(0x7FFFFFFF)), f32)
    diag = lax.broadcasted_iota(jnp.int32, (tq, tq), 0) == lax.broadcasted_iota(jnp.int32, (tq, tq), 1)
    thr_col = jnp.sum(jnp.where(diag, thr_row, 0.0), axis=-1, keepdims=True)
    selected = (score_ref[:, 0:s_len] >= thr_col) & causal

    sel_g = jnp.concatenate([selected] * group, axis=0)
    probs, denoms = [], []
    for kv in range(ATT_KV_HEADS):
        s = jnp.where(sel_g, logit_ref[kv, :, 0:s_len], -jnp.inf)
        e = jnp.exp2((s - jnp.max(s, axis=-1, keepdims=True)) * (ATT_HEAD ** -0.5 * _LOG2_E))
        denoms.append(jnp.sum(e, axis=-1, keepdims=True))
        probs.append(e.astype(bf16))
    for kv in range(ATT_KV_HEADS):
        o = _mm(probs[kv], v_ref[0:s_len, kv * LANES:(kv + 1) * LANES]) / denoms[kv]
        for g in range(group):
            hq = kv * group + g
            o_ref[:, hq * LANES:(hq + 1) * LANES] = o[g * tq:(g + 1) * tq].astype(o_ref.dtype)


def _dsa(q, qi, wi, k, v, ki2, batch, seq, tq):
    nq = seq // tq
    n_sel = min(TOPK_MAX, seq // 4)
    qblk = lambda width: pl.BlockSpec((tq, width), lambda b, i: (b * nq + i, 0))
    kblk = lambda width: pl.BlockSpec((seq, width), lambda b, i: (b, 0))
    kvw = ATT_KV_HEADS * ATT_HEAD
    bucket_len = max(tq, seq // DSA_BUCKETS)
    return pl.pallas_call(
        functools.partial(_dsa_kernel, n_sel=n_sel, bucket_len=bucket_len),
        grid=(batch, nq),
        in_specs=[qblk(ATT_WIDTH), qblk(IDX_HEADS * IDX_HEAD), qblk(LANES),
                  kblk(kvw), kblk(kvw), kblk(LANES)],
        out_specs=qblk(ATT_WIDTH),
        out_shape=jax.ShapeDtypeStruct((batch * seq, ATT_WIDTH), bf16),
        scratch_shapes=[pltpu.VMEM((tq, seq), f32),
                        pltpu.VMEM((seq // SUBLANES, SUBLANES, tq), jnp.int32),
                        pltpu.VMEM((ATT_KV_HEADS, (ATT_HEADS // ATT_KV_HEADS) * tq, seq), f32)],
        compiler_params=_params("parallel", "arbitrary"),
        name="dsa_attention",
    )(q, qi, wi, k, v, ki2)


def _mix_cross_kernel(x_ref, ya_ref, yb_ref, wa_ref, wb_ref, g_ref, kv_ref, wq_ref, wo_ref, o_ref):
    h = x_ref[...] + _mm(ya_ref[...], wa_ref[...]) + _mm(yb_ref[...], wb_ref[...])
    hn = _rms(h, g_ref[...]).astype(bf16)
    q = _mm(hn, wq_ref[...]).astype(bf16)
    width = CROSS_HEADS * CROSS_HEAD
    cols = lambda hd, base=0: slice(base + hd * CROSS_HEAD, base + (hd + 1) * CROSS_HEAD)
    logits = [_nt(q[:, cols(hd)], kv_ref[:, cols(hd)]) for hd in range(CROSS_HEADS)]
    exps = [jnp.exp2((s - jnp.max(s, axis=-1, keepdims=True)) * (CROSS_HEAD ** -0.5 * _LOG2_E)) for s in logits]
    outs = [(_mm(e.astype(bf16), kv_ref[:, cols(hd, width)]) / jnp.sum(e, axis=-1, keepdims=True)).astype(bf16)
            for hd, e in enumerate(exps)]
    o = jnp.concatenate(outs, axis=-1)
    o_ref[...] = h + _mm(o, wo_ref[...])


def _mix_cross(x, ya, yb, w_mix, g, kv, wq, wo, batch, seq, n_mem, tm):
    d = x.shape[1]
    ka, kb = ya.shape[1], yb.shape[1]
    ns = seq // tm
    width = CROSS_HEADS * CROSS_HEAD
    rows = lambda cols: pl.BlockSpec((tm, cols), lambda b, i: (b * ns + i, 0))
    const = lambda shape, idx: pl.BlockSpec(shape, lambda b, i: idx, pipeline_mode=pl.Buffered(1))
    return pl.pallas_call(
        _mix_cross_kernel,
        grid=(batch, ns),
        in_specs=[rows(d), rows(ka), rows(kb),
                  const((ka, d), (0, 0)), const((kb, d), (ka // kb, 0)),
                  pl.BlockSpec((1, d), lambda b, i: (0, 0)),
                  pl.BlockSpec((n_mem, 2 * width), lambda b, i: (b, 0)),
                  const((d, width), (0, 0)), const((width, d), (0, 0))],
        out_specs=rows(d),
        out_shape=jax.ShapeDtypeStruct(x.shape, f32),
        compiler_params=_params("parallel", "arbitrary"),
        name="mix_out_cross_attention",
    )(x, ya, yb, w_mix, w_mix, g, kv, wq, wo)


def _mlp_kernel(h_ref, g_ref, wu_ref, wd_ref, gf_ref, o_ref, hn_ref, *, final_norm):
    j = pl.program_id(1)

    @pl.when(j == 0)
    def _():
        h = h_ref[...]
        hn_ref[...] = _rms(h, g_ref[...]).astype(bf16)
        o_ref[...] = h

    u = jnp.maximum(_mm(hn_ref[...], wu_ref[...]), 0.0)
    o_ref[...] += _mm((u * u).astype(bf16), wd_ref[...])

    if final_norm:
        @pl.when(j == pl.num_programs(1) - 1)
        def _():
            o_ref[...] = _rms(o_ref[...], gf_ref[...])


def _mlp(h, g, wu, wd, gf, final_norm, tm, tf):
    t, d = h.shape
    dff = wu.shape[1]
    return pl.pallas_call(
        functools.partial(_mlp_kernel, final_norm=final_norm),
        grid=(t // tm, dff // tf),
        in_specs=[pl.BlockSpec((tm, d), lambda i, j: (i, 0)),
                  pl.BlockSpec((1, d), lambda i, j: (0, 0)),
                  pl.BlockSpec((d, tf), lambda i, j: (0, j)),
                  pl.BlockSpec((tf, d), lambda i, j: (j, 0)),
                  pl.BlockSpec((1, d), lambda i, j: (0, 0))],
        out_specs=pl.BlockSpec((tm, d), lambda i, j: (i, 0)),
        out_shape=jax.ShapeDtypeStruct((t, d), f32),
        scratch_shapes=[pltpu.VMEM((tm, d), bf16)],
        compiler_params=_params("parallel", "arbitrary"),
        name="mlp_final_norm",
    )(h, g, wu, wd, gf)


def _pack_in_proj(w_in, mu):
    d = w_in.shape[0]
    w_t = jnp.swapaxes(w_in, 0, 1)
    zeros = lambda n: jnp.zeros((n, d), w_in.dtype)
    w = jnp.concatenate([w_t[:RWKV_COLS], zeros(COL_Q - RWKV_COLS),
                         w_t[RWKV_COLS:], zeros(IN_COLS_PACKED - COL_Q - ATT_COLS)], axis=0).astype(bf16)
    mu_lo = jnp.concatenate([mu[COL_LORA:RWKV_COLS], jnp.zeros((COL_Q - RWKV_COLS,), mu.dtype)])
    mus = (mu[0:1024][None], mu[1024:2048][None], mu[2048:3072][None], mu_lo[None])
    return w, mus


def _rope_freq():
    def inv_freq(head):
        half = head // 8
        return ROPE_THETA ** (-(lane % half).astype(f32) / half)
    lane = jnp.arange(LANES)
    att = inv_freq(ATT_HEAD)
    idx = inv_freq(IDX_HEAD)
    in_idx = (lane >= IDX_HEAD) & (lane < IDX_HEAD + IDX_HEAD // 4)
    return jnp.where(lane < ATT_HEAD // 4, att, jnp.where(in_idx, idx, 0.0)).astype(f32)[None]


def kernel(x, mem, positions, norm_mix, w_in, rwkv_mu, w_decay0, w_decay_up, a0, a_up, g_up, k_k, k_a, r_k,
           lnx_w, lnx_b, w_mix_out, norm_cross, norm_mem, w_q_cross, w_kv_cross, w_o_cross, norm_mlp,
           w_up, w_down, norm_final):
    batch, seq, d = x.shape
    n_mem = mem.shape[1]
    t = batch * seq
    depth = w_in.shape[0]
    tm = min(512, t)
    h = x.reshape(t, d)
    pos = positions.reshape(t, 1).astype(f32)
    mem2 = mem.reshape(batch * n_mem, d)
    row = lambda vct: vct.reshape(1, -1)

    for l in range(depth):
        w_in_p, mus = _pack_in_proj(w_in[l], rwkv_mu[l])
        zpad = jnp.zeros((DECAY_LORA, RWKV_WIDTH), f32)
        w_wa = jnp.concatenate([jnp.concatenate([w_decay_up[l], zpad], axis=1),
                                jnp.concatenate([zpad, a_up[l]], axis=1)], axis=0).astype(bf16)
        g_up_p = jnp.concatenate([g_up[l], jnp.zeros((LORA_BLOCK - LANES - GATE_LORA, RWKV_WIDTH), f32)],
                                 axis=0).astype(bf16)

        p = _norm_matmul(h, row(norm_mix[l]), w_in_p, f32, min(1024, t), 2048, w_is_transposed=True)
        y_rwkv = _rwkv(p, batch, seq, mus, row(w_decay0[l]), row(a0[l]), row(k_k[l]), row(k_a[l]),
                       row(lnx_w[l]), row(lnx_b[l]), row(r_k[l]), w_wa, g_up_p)
        q, qi, k_att, v_att, ki2, wi = _dsa_prep(p, pos, _rope_freq(), tm)
        y_att = _dsa(q, qi, wi, k_att, v_att, ki2, batch, seq, min(DSA_QUERY_BLOCK, seq))
        kv = _norm_matmul(mem2, row(norm_mem[l]), w_kv_cross[l].astype(bf16), bf16, min(512, batch * n_mem), 1024)
        h = _mix_cross(h, y_rwkv, y_att, w_mix_out[l].astype(bf16), row(norm_cross[l]), kv,
                       w_q_cross[l].astype(bf16), w_o_cross[l].astype(bf16), batch, seq, n_mem, tm)
        h = _mlp(h, row(norm_mlp[l]), w_up[l].astype(bf16), w_down[l].astype(bf16), row(norm_final),
                 l == depth - 1, tm, 2048)
    return h.reshape(batch, seq, d)
```
